```python
import math
import jax
import jax.numpy as jnp
from jax import lax
import numpy as np

D_MODEL = 1024
BATCH = 32
SEQ = 2048
DEPTH = 4

GRID_W = 64
CTX_LEN = 256
N_MOD = 6
EPS = 1e-6
W_A = 256
W_B = 256
W_C = 256
W_D = 256
D_MIX = W_A + W_B + W_C + W_D
LRU_HEADS = 4
LRU_HEAD_DIM = W_A // LRU_HEADS
LRU_CONV = 4
RG_C = 8.0
DIFF_HEADS = 4
DIFF_DV = W_B // DIFF_HEADS
DIFF_DK = DIFF_DV // 2
ROPE_BASE = 10000.0
ROPE_FREQS = DIFF_DK // 4
Q_BLOCK = 128
CONF_K = 31
HYENA_ORDER = 2
HYENA_SHORT = 3
HYENA_EMB = 33
HYENA_BANDS = (HYENA_EMB - 1) // 2
HYENA_FFN = 64
HYENA_MIN_DECAY = -math.log(1e-2) / 1.5
HYENA_MAX_DECAY = -math.log(1e-2) / 0.3
N_GROUPS = 4
EXPERTS_PER_GROUP = 8
N_EXPERTS = N_GROUPS * EXPERTS_PER_GROUP
TOP_K = 2
D_EXPERT = 512
MOE_BLOCK = 128
IN_A = 2 * W_A
IN_B = 3 * W_B
IN_C = 2 * W_C
IN_D = (HYENA_ORDER + 1) * W_D
OFF_B = IN_A
OFF_C = OFF_B + IN_B
OFF_D = OFF_C + IN_C
D_IN = OFF_D + IN_D

kernel_name = 'hybrid_parallel_groups_dit_block'


def rms_norm(x, g, eps=EPS):
    xf = x.astype(jnp.float32)
    y = xf * lax.rsqrt(jnp.mean(xf * xf, -1, keepdims=True) + eps)
    return (y * g).astype(x.dtype)


def layer_norm(x, g, b, eps=1e-5):
    xf = x.astype(jnp.float32)
    mu = jnp.mean(xf, -1, keepdims=True)
    var = jnp.mean(jnp.square(xf - mu), -1, keepdims=True)
    return ((xf - mu) * lax.rsqrt(var + eps) * g + b).astype(x.dtype)


def modulate(x, shift, scale):
    return x * (1.0 + scale) + shift


def dwconv(x, w, b, pad):
    y = lax.conv_general_dilated(
        x, w.astype(x.dtype)[:, None, :], window_strides=(1,), padding=[pad],
        dimension_numbers=('NWC', 'WIO', 'NWC'), feature_group_count=x.shape[-1])
    return y + b


def linear_scan(a, b, h0, reverse):
    edge = -1 if reverse else 0
    b = b.at[:, edge].add(a[:, edge] * h0)

    def combine(left, right):
        a1, b1 = left
        a2, b2 = right
        return a1 * a2, a2 * b1 + b2

    _, h = lax.associative_scan(combine, (a, b), reverse=reverse, axis=1)
    return h


def rglru_direction(x, conv_w, conv_b, w_r, b_r, w_i, b_i, lam, pad, reverse, h0):
    u = dwconv(x, conv_w, conv_b, pad)
    B_, L, _ = u.shape
    uh = u.reshape(B_, L, LRU_HEADS, LRU_HEAD_DIM)
    r = jax.nn.sigmoid((jnp.einsum('blhi,hij->blhj', uh, w_r).reshape(B_, L, W_A) + b_r).astype(jnp.float32))
    i = jax.nn.sigmoid((jnp.einsum('blhi,hij->blhj', uh, w_i).reshape(B_, L, W_A) + b_i).astype(jnp.float32))
    log_a = -RG_C * r * jax.nn.softplus(-lam.astype(jnp.float32))
    a = jnp.exp(log_a)
    b = jnp.sqrt(-jnp.expm1(2.0 * log_a)) * (i * u.astype(jnp.float32))
    return linear_scan(a, b, h0, reverse)


def rglru_mixer(pa, pa_c, conv_w, conv_b, w_r, b_r, w_i, b_i, lam, ctx_out):
    xr, xg = pa[..., :W_A], pa[..., W_A:]
    xr_c, xg_c = pa_c[..., :W_A], pa_c[..., W_A:]
    h_lat, h_ctx = [], []
    for d, reverse in enumerate((False, True)):
        pad = (0, LRU_CONV - 1) if reverse else (LRU_CONV - 1, 0)
        prm = (conv_w[d], conv_b[d], w_r[d], b_r[d], w_i[d], b_i[d], lam[d], pad, reverse)
        hc = rglru_direction(xr_c, *prm, jnp.zeros((xr_c.shape[0], W_A), jnp.float32))
        h_final = hc[:, 0] if reverse else hc[:, -1]
        h_lat.append(rglru_direction(xr, *prm, h_final))
        h_ctx.append(hc)
    y = (h_lat[0] + h_lat[1]) * jax.nn.gelu(xg)
    y_c = (h_ctx[0] + h_ctx[1]) * jax.nn.gelu(xg_c) if ctx_out else None
    return y, y_c


def axial_rope_tables(L):
    rows = L // GRID_W
    row = jnp.repeat(jnp.arange(rows, dtype=jnp.float32), GRID_W)
    col = jnp.tile(jnp.arange(GRID_W, dtype=jnp.float32), rows)
    inv = ROPE_BASE ** (-jnp.arange(ROPE_FREQS, dtype=jnp.float32) / ROPE_FREQS)
    ang = jnp.concatenate([row[:, None] * inv, col[:, None] * inv], -1)
    return jnp.cos(ang), jnp.sin(ang)


def apply_rope(x, cos, sin):
    half = x.shape[-1] // 2
    cs = cos[None, :, None, None, :]
    sn = sin[None, :, None, None, :]
    x1, x2 = x[..., :half], x[..., half:]
    return jnp.concatenate([x1 * cs - x2 * sn, x1 * sn + x2 * cs], -1).astype(x.dtype)


def diff_softmax(q, k, v, lam):
    s = jnp.einsum('bqhcd,bkhcd->bhcqk', q, k).astype(jnp.float32) * (DIFF_DK ** -0.5)
    p = jax.nn.softmax(s, -1)
    w = p[:, :, 0] - lam * p[:, :, 1]
    return jnp.einsum('bhqk,bkhd->bqhd', w, v)


def diff_attn_mixer(pb, pb_c, rope_cos, rope_sin, lq1, lk1, lq2, lk2, sub_g, lam_init, ctx_out):
    B_, L, _ = pb.shape
    Lc = pb_c.shape[1]
    lam = (jnp.exp(jnp.sum(lq1.astype(jnp.float32) * lk1)) -
           jnp.exp(jnp.sum(lq2.astype(jnp.float32) * lk2)) + lam_init)

    def split_qkv(pp, n):
        q, k, v = jnp.split(pp, 3, -1)
        return (q.reshape(B_, n, DIFF_HEADS, 2, DIFF_DK), k.reshape(B_, n, DIFF_HEADS, 2, DIFF_DK),
                v.reshape(B_, n, DIFF_HEADS, DIFF_DV))

    q, k, v = split_qkv(pb, L)
    q = apply_rope(q, rope_cos, rope_sin)
    k = apply_rope(k, rope_cos, rope_sin)
    q_c, k_c, v_c = split_qkv(pb_c, Lc)
    keys = jnp.concatenate([k, k_c], 1)
    vals = jnp.concatenate([v, v_c], 1)
    n_blk = L // Q_BLOCK
    qb = jnp.moveaxis(q.reshape(B_, n_blk, Q_BLOCK, DIFF_HEADS, 2, DIFF_DK), 1, 0)
    o = lax.map(lambda qq: diff_softmax(qq, keys, vals, lam), qb)
    o = jnp.moveaxis(o, 0, 1).reshape(B_, L, DIFF_HEADS, DIFF_DV)

    def head_out(oo):
        return (rms_norm(oo, sub_g) * (1.0 - lam_init)).reshape(oo.shape[0], oo.shape[1], W_B)

    y = head_out(o)
    y_c = head_out(diff_softmax(q_c, k_c, v_c, lam)) if ctx_out else None
    return y, y_c


def conformer_conv(pc, conv_w, conv_b, ln_g, ln_b, w_pw, b_pw):
    u = pc[..., :W_C] * jax.nn.sigmoid(pc[..., W_C:])
    u = dwconv(u, conv_w, conv_b, (CONF_K // 2, CONF_K // 2))
    u = layer_norm(u, ln_g, ln_b)
    return jax.nn.silu(u) @ w_pw + b_pw


def hyena_filters(L, w_f1, b_f1, freq, w_f2, b_f2, w_f3, decay):
    t = jnp.arange(L, dtype=jnp.float32)
    t_unit = t / max(L - 1, 1)
    bands = jnp.linspace(1e-4, HYENA_BANDS - 1, HYENA_BANDS, dtype=jnp.float32)
    ang = (2.0 * math.pi / L) * t[:, None] * bands[None, :]
    feats = jnp.concatenate([t_unit[:, None], jnp.cos(ang), -jnp.sin(ang)], -1)
    f = jnp.sin(freq * (feats @ w_f1 + b_f1))
    f = jnp.sin(freq * (f @ w_f2 + b_f2))
    h = (f @ w_f3) * jnp.exp(-t_unit[:, None] * jnp.abs(decay))
    h = h.astype(jnp.float32).reshape(L, HYENA_ORDER, 2, W_D)
    energy = jnp.sum(jnp.square(h[:, :, 0]), 0) + jnp.sum(jnp.square(h[1:, :, 1]), 0)
    return h * lax.rsqrt(energy + EPS)[None, :, None, :]


def long_conv_bidir(z, h_fwd, h_bwd, bias):
    B_, L, C = z.shape
    n = 2 * L
    taps = jnp.concatenate([h_fwd, jnp.zeros((1, C), jnp.float32), h_bwd[:0:-1]], 0)
    zf = z.astype(jnp.float32)
    y = jnp.fft.irfft(jnp.fft.rfft(zf, n=n, axis=1) * jnp.fft.rfft(taps, n=n, axis=0)[None], n=n, axis=1)
    return y[:, :L] + zf * bias


def hyena_mixer(pd, filt, conv_w, conv_b, h_bias):
    u = dwconv(pd, conv_w, conv_b, (HYENA_SHORT // 2, HYENA_SHORT // 2))
    v, x1, x2 = jnp.split(u, HYENA_ORDER + 1, -1)
    z = v
    for o, gate in enumerate((x1, x2)):
        z = gate * long_conv_bidir(z, filt[:, o, 0], filt[:, o, 1], h_bias[o])
    return z


def hier_moe(h, w_rg, b_rg, w_re, b_re, w_gate, w_up, w_down):
    T, D = h.shape
    hf = h.astype(jnp.float32)
    p_grp, g_idx = lax.top_k(jax.nn.softmax(hf @ w_rg.astype(jnp.float32) + b_rg, -1), 1)
    le = (hf @ w_re.astype(jnp.float32) + b_re).reshape(T, N_GROUPS, EXPERTS_PER_GROUP)
    le = jnp.take_along_axis(le, g_idx[:, :, None], axis=1)[:, 0]
    p_in, e_in = lax.top_k(jax.nn.softmax(le, -1), TOP_K)
    p_in = p_in / jnp.sum(p_in, -1, keepdims=True)
    gate = (p_grp * p_in).reshape(-1)
    expert = (g_idx * EXPERTS_PER_GROUP + e_in).reshape(-1)
    token = jnp.repeat(jnp.arange(T, dtype=jnp.int32), TOP_K)
    n_assign = T * TOP_K
    order = jnp.argsort(expert)
    e_sorted = expert[order]
    counts = jnp.bincount(expert, length=N_EXPERTS)
    padded = (counts + MOE_BLOCK - 1) // MOE_BLOCK * MOE_BLOCK
    start = jnp.cumsum(counts) - counts
    p_end = jnp.cumsum(padded)
    p_start = p_end - padded
    dest = p_start[e_sorted] + jnp.arange(n_assign, dtype=jnp.int32) - start[e_sorted]
    n_slots = -(-n_assign // MOE_BLOCK) * MOE_BLOCK + N_EXPERTS * MOE_BLOCK
    n_blk = n_slots // MOE_BLOCK
    slot_tok = jnp.full((n_slots,), T, jnp.int32).at[dest].set(token[order])
    slot_gate = jnp.zeros((n_slots,), jnp.float32).at[dest].set(gate[order])
    blk_expert = jnp.minimum(
        jnp.searchsorted(p_end, jnp.arange(n_blk, dtype=jnp.int32) * MOE_BLOCK, side='right'), N_EXPERTS - 1)
    h_pad = jnp.concatenate([h, jnp.zeros((1, D), h.dtype)], 0)

    def expert_block(args):
        toks, gw, e = args
        xb = h_pad[toks]
        hid = jax.nn.silu(xb @ w_gate[e]) * (xb @ w_up[e])
        return (hid @ w_down[e]) * gw[:, None]

    y = lax.map(expert_block, (slot_tok.reshape(n_blk, MOE_BLOCK),
                               slot_gate.reshape(n_blk, MOE_BLOCK), blk_expert))
    y = jax.ops.segment_sum(y.reshape(n_slots, D), slot_tok, num_segments=T + 1)
    return y[:T]


def setup_inputs(seed: int = 0) -> dict:
    key = jax.random.key(seed)
    ks = iter(jax.random.split(key, 64))

    def nrm(shape, scale):
        return scale * jax.random.normal(next(ks), shape, jnp.float32)

    def gain(shape):
        return 1.0 + nrm(shape, 0.05)

    x = nrm((BATCH, SEQ, D_MODEL), 1.0)
    c = nrm((BATCH, D_MODEL), 1.0)
    ctx = nrm((BATCH, CTX_LEN, D_MODEL), 1.0)
    c_ctx = nrm((D_MODEL,), 1.0)
    w_ada = nrm((DEPTH, D_MODEL, N_MOD * D_MODEL), 0.5 * D_MODEL ** -0.5)
    b_ada = nrm((DEPTH, N_MOD * D_MODEL), 0.02)
    g_mix = gain((DEPTH, D_MODEL))
    g_ffn = gain((DEPTH, D_MODEL))
    w_in = nrm((DEPTH, D_MODEL, D_IN), D_MODEL ** -0.5)
    w_out = nrm((DEPTH, D_MIX, D_MODEL), D_MIX ** -0.5)
    a_conv_w = nrm((DEPTH, 2, LRU_CONV, W_A), LRU_CONV ** -0.5)
    a_conv_b = nrm((DEPTH, 2, W_A), 0.02)
    a_w_r = nrm((DEPTH, 2, LRU_HEADS, LRU_HEAD_DIM, LRU_HEAD_DIM), LRU_HEAD_DIM ** -0.5)
    a_b_r = nrm((DEPTH, 2, W_A), 0.02)
    a_w_i = nrm((DEPTH, 2, LRU_HEADS, LRU_HEAD_DIM, LRU_HEAD_DIM), LRU_HEAD_DIM ** -0.5)
    a_b_i = nrm((DEPTH, 2, W_A), 0.02)
    a_pow = jax.random.uniform(next(ks), (DEPTH, 2, W_A), jnp.float32, 0.9, 0.999)
    a_base = a_pow ** (1.0 / RG_C)
    a_lam = jnp.log(a_base) - jnp.log1p(-a_base)
    b_lq1 = nrm((DEPTH, DIFF_DK), 0.1)
    b_lk1 = nrm((DEPTH, DIFF_DK), 0.1)
    b_lq2 = nrm((DEPTH, DIFF_DK), 0.1)
    b_lk2 = nrm((DEPTH, DIFF_DK), 0.1)
    b_sub_g = gain((DEPTH, DIFF_DV))
    c_conv_w = nrm((DEPTH, CONF_K, W_C), CONF_K ** -0.5)
    c_conv_b = nrm((DEPTH, W_C), 0.02)
    c_ln_g = gain((DEPTH, W_C))
    c_ln_b = nrm((DEPTH, W_C), 0.02)
    c_w_pw = nrm((DEPTH, W_C, W_C), W_C ** -0.5)
    c_b_pw = nrm((DEPTH, W_C), 0.02)
    d_conv_w = nrm((DEPTH, HYENA_SHORT, IN_D), HYENA_SHORT ** -0.5)
    d_conv_b = nrm((DEPTH, IN_D), 0.02)
    d_w_f1 = nrm((DEPTH, HYENA_EMB, HYENA_FFN), HYENA_EMB ** -0.5)
    d_b_f1 = nrm((DEPTH, HYENA_FFN), 0.1)
    d_freq = gain((DEPTH, HYENA_FFN))
    d_w_f2 = nrm((DEPTH, HYENA_FFN, HYENA_FFN), HYENA_FFN ** -0.5)
    d_b_f2 = nrm((DEPTH, HYENA_FFN), 0.1)
    d_w_f3 = nrm((DEPTH, HYENA_FFN, 2 * HYENA_ORDER * W_D), HYENA_FFN ** -0.5)
    decay0 = jnp.tile(jnp.linspace(HYENA_MIN_DECAY, HYENA_MAX_DECAY, W_D, dtype=jnp.float32), 2 * HYENA_ORDER)
    d_decay = decay0[None, :] + nrm((DEPTH, 2 * HYENA_ORDER * W_D), 0.1)
    d_bias = nrm((DEPTH, HYENA_ORDER, W_D), 1.0)
    moe_w_rg = nrm((DEPTH, D_MODEL, N_GROUPS), D_MODEL ** -0.5)
    moe_b_rg = nrm((DEPTH, N_GROUPS), 0.01)
    moe_w_re = nrm((DEPTH, D_MODEL, N_EXPERTS), D_MODEL ** -0.5)
    moe_b_re = nrm((DEPTH, N_EXPERTS), 0.01)
    moe_w_gate = nrm((DEPTH, N_EXPERTS, D_MODEL, D_EXPERT), D_MODEL ** -0.5)
    moe_w_up = nrm((DEPTH, N_EXPERTS, D_MODEL, D_EXPERT), D_MODEL ** -0.5)
    moe_w_down = nrm((DEPTH, N_EXPERTS, D_EXPERT, D_MODEL), D_EXPERT ** -0.5)
    g_final = gain((D_MODEL,))
    return {
        'x': x, 'c': c, 'ctx': ctx, 'c_ctx': c_ctx,
        'w_ada': w_ada, 'b_ada': b_ada, 'g_mix': g_mix, 'g_ffn': g_ffn, 'w_in': w_in, 'w_out': w_out,
        'a_conv_w': a_conv_w, 'a_conv_b': a_conv_b, 'a_w_r': a_w_r, 'a_b_r': a_b_r,
        'a_w_i': a_w_i, 'a_b_i': a_b_i, 'a_lam': a_lam,
        'b_lq1': b_lq1, 'b_lk1': b_lk1, 'b_lq2': b_lq2, 'b_lk2': b_lk2, 'b_sub_g': b_sub_g,
        'c_conv_w': c_conv_w, 'c_conv_b': c_conv_b, 'c_ln_g': c_ln_g, 'c_ln_b': c_ln_b,
        'c_w_pw': c_w_pw, 'c_b_pw': c_b_pw,
        'd_conv_w': d_conv_w, 'd_conv_b': d_conv_b, 'd_w_f1': d_w_f1, 'd_b_f1': d_b_f1, 'd_freq': d_freq,
        'd_w_f2': d_w_f2, 'd_b_f2': d_b_f2, 'd_w_f3': d_w_f3, 'd_decay': d_decay, 'd_bias': d_bias,
        'moe_w_rg': moe_w_rg, 'moe_b_rg': moe_b_rg, 'moe_w_re': moe_w_re, 'moe_b_re': moe_b_re,
        'moe_w_gate': moe_w_gate, 'moe_w_up': moe_w_up, 'moe_w_down': moe_w_down,
        'g_final': g_final,
    }


def reference(x, c, ctx, c_ctx, w_ada, b_ada, g_mix, g_ffn, w_in, w_out,
              a_conv_w, a_conv_b, a_w_r, a_b_r, a_w_i, a_b_i, a_lam,
              b_lq1, b_lk1, b_lq2, b_lk2, b_sub_g,
              c_conv_w, c_conv_b, c_ln_g, c_ln_b, c_w_pw, c_b_pw,
              d_conv_w, d_conv_b, d_w_f1, d_b_f1, d_freq, d_w_f2, d_b_f2, d_w_f3, d_decay, d_bias,
              moe_w_rg, moe_b_rg, moe_w_re, moe_b_re, moe_w_gate, moe_w_up, moe_w_down,
              g_final):
    out_dtype = x.dtype
    B_, L, D = x.shape
    Lc = ctx.shape[1]
    n_lat = B_ * L
    rope_cos, rope_sin = axial_rope_tables(L)
    s_lat = jax.nn.silu(c.astype(jnp.float32))
    s_ctx = jax.nn.silu(c_ctx.astype(jnp.float32))
    xc = ctx
    for l in range(DEPTH):
        last = l == DEPTH - 1
        mod = jnp.split((s_lat @ w_ada[l] + b_ada[l])[:, None, :], N_MOD, -1)
        mod_c = jnp.split(s_ctx @ w_ada[l] + b_ada[l], N_MOD, -1)
        lam_init = 0.8 - 0.6 * math.exp(-0.3 * l)
        conf = (c_conv_w[l], c_conv_b[l], c_ln_g[l], c_ln_b[l], c_w_pw[l], c_b_pw[l])
        hy = (d_w_f1[l], d_b_f1[l], d_freq[l], d_w_f2[l], d_b_f2[l], d_w_f3[l], d_decay[l])
        h = modulate(rms_norm(x, g_mix[l]), mod[0], mod[1])
        hc = modulate(rms_norm(xc, g_mix[l]), mod_c[0], mod_c[1])
        p = h @ w_in[l]
        p_c = hc @ w_in[l][:, :(OFF_C if last else D_IN)]
        y_a, yc_a = rglru_mixer(p[..., :OFF_B], p_c[..., :OFF_B], a_conv_w[l], a_conv_b[l],
                                a_w_r[l], a_b_r[l], a_w_i[l], a_b_i[l], a_lam[l], not last)
        y_b, yc_b = diff_attn_mixer(p[..., OFF_B:OFF_C], p_c[..., OFF_B:OFF_C], rope_cos, rope_sin,
                                    b_lq1[l], b_lk1[l], b_lq2[l], b_lk2[l], b_sub_g[l], lam_init, not last)
        y_c = conformer_conv(p[..., OFF_C:OFF_D], *conf)
        y_d = hyena_mixer(p[..., OFF_D:], hyena_filters(L, *hy), d_conv_w[l], d_conv_b[l], d_bias[l])
        x = x + mod[2] * (jnp.concatenate([y_a, y_b, y_c, y_d], -1) @ w_out[l])
        if not last:
            yc_c = conformer_conv(p_c[..., OFF_C:OFF_D], *conf)
            yc_d = hyena_mixer(p_c[..., OFF_D:], hyena_filters(Lc, *hy), d_conv_w[l], d_conv_b[l], d_bias[l])
            xc = xc + mod_c[2] * (jnp.concatenate([yc_a, yc_b, yc_c, yc_d], -1) @ w_out[l])
        moe = (moe_w_rg[l], moe_b_rg[l], moe_w_re[l], moe_b_re[l], moe_w_gate[l], moe_w_up[l], moe_w_down[l])
        h = modulate(rms_norm(x, g_ffn[l]), mod[3], mod[4]).reshape(n_lat, D)
        if last:
            x = x + mod[5] * hier_moe(h, *moe).reshape(B_, L, D)
        else:
            hc = modulate(rms_norm(xc, g_ffn[l]), mod_c[3], mod_c[4]).reshape(B_ * Lc, D)
            f = hier_moe(jnp.concatenate([h, hc], 0), *moe)
            x = x + mod[5] * f[:n_lat].reshape(B_, L, D)
            xc = xc + mod_c[5] * f[n_lat:].reshape(B_, Lc, D)
    return rms_norm(x, g_final).astype(out_dtype)
```

```python
import functools
import math

import numpy as np
import jax
import jax.numpy as jnp
from jax import lax
from jax.experimental import pallas as pl
from jax.experimental.pallas import tpu as pltpu

F32 = jnp.float32
BF16 = jnp.bfloat16

D_MODEL = 1024
N_MOD = 6
EPS = 1e-6
W_MIX = 256
LRU_HEADS = 4
LRU_CONV = 4
RG_C = 8.0
DIFF_HEADS = 4
DIFF_DV = W_MIX // DIFF_HEADS
DIFF_DK = DIFF_DV // 2
ROPE_BASE = 10000.0
ROPE_FREQS = DIFF_DK // 4
GRID_W = 64
CONF_K = 31
HYENA_ORDER = 2
HYENA_SHORT = 3
HYENA_EMB = 33
HYENA_BANDS = (HYENA_EMB - 1) // 2
HYENA_FFN = 64
N_GROUPS = 4
EXPERTS_PER_GROUP = 8
N_EXPERTS = N_GROUPS * EXPERTS_PER_GROUP
D_EXPERT = 512
IN_A, IN_B, IN_C, IN_D = 2 * W_MIX, 3 * W_MIX, 2 * W_MIX, 3 * W_MIX
D_IN = IN_A + IN_B + IN_C + IN_D

LANES = 128
TOKEN_TILE = 768
Q_TILE = 256
CONV_BLOCK = 512
CONV_FFT = 2 * CONV_BLOCK
EXPERT_TILE = 256
MOD_ROWS_PAD = 8
VMEM_LIMIT = 56 * 1024 * 1024


def _cparams(sem, vmem=VMEM_LIMIT):
    return pltpu.CompilerParams(dimension_semantics=sem, vmem_limit_bytes=vmem)


def _dot(a, b):
    return jnp.dot(a, b, preferred_element_type=F32)


def _split_bf16(a):
    hi = a.astype(BF16)
    lo = (a - hi.astype(F32)).astype(BF16)
    return hi, lo


def _dot3(a, b):
    ah, al = _split_bf16(a)
    bh, bl = _split_bf16(b)
    return _dot(ah, bh) + _dot(al, bh) + _dot(ah, bl)


def _dot2_lhs(a, b_bf16):
    ah, al = _split_bf16(a)
    return _dot(ah, b_bf16) + _dot(al, b_bf16)


def _dot2_rhs(a_bf16, b):
    bh, bl = _split_bf16(b)
    return _dot(a_bf16, bh) + _dot(a_bf16, bl)


def _sigmoid(x):
    return 1.0 / (1.0 + jnp.exp(-x))


def _silu(x):
    return x * _sigmoid(x)


def _gelu_tanh(x):
    return 0.5 * x * (1.0 + jnp.tanh(math.sqrt(2.0 / math.pi) * (x + 0.044715 * (x * x * x))))


def _is_ctx_rows(tile_idx, tile_rows, n_lat):
    rows = tile_idx * tile_rows + lax.broadcasted_iota(jnp.int32, (tile_rows, 1), 0)
    return rows >= n_lat


def _mod_vec(mb_ref, mc_ref, k, is_ctx):
    vb = mb_ref[0, :, k * D_MODEL:(k + 1) * D_MODEL]
    vc = mc_ref[0, :, k * D_MODEL:(k + 1) * D_MODEL]
    return jnp.where(is_ctx, vc, vb)


def _rms_modulate(x, g, shift, scale):
    y = x * lax.rsqrt(jnp.mean(x * x, -1, keepdims=True) + EPS)
    return (y * g) * (1.0 + scale) + shift


def _ada_kernel(c_ref, w_ref, b_ref, o_ref):
    o_ref[0] = _dot3(_silu(c_ref[...]), w_ref[0]) + b_ref[0]


def _ada_table(c_all, w_ada, b_ada):
    depth, d, n = w_ada.shape
    rows = c_all.shape[0]
    tn = 1536
    return pl.pallas_call(
        _ada_kernel,
        out_shape=jax.ShapeDtypeStruct((depth, rows, n), F32),
        grid=(depth, n // tn),
        in_specs=[pl.BlockSpec((rows, d), lambda l, j: (0, 0)),
                  pl.BlockSpec((1, d, tn), lambda l, j: (l, 0, j)),
                  pl.BlockSpec((1, 1, tn), lambda l, j: (l, 0, j))],
        out_specs=pl.BlockSpec((1, rows, tn), lambda l, j: (l, 0, j)),
        compiler_params=_cparams(("arbitrary", "arbitrary")),
        name="ada_table",
    )(c_all, w_ada, b_ada.reshape(depth, 1, n))


def _in_proj_kernel(x_ref, mb_ref, mc_ref, g_ref, w_ref, pa_ref, pb_ref, pc_ref, pd_ref, *, n_lat):
    is_ctx = _is_ctx_rows(pl.program_id(1), TOKEN_TILE, n_lat)
    h = _rms_modulate(x_ref[0], g_ref[...], _mod_vec(mb_ref, mc_ref, 0, is_ctx),
                      _mod_vec(mb_ref, mc_ref, 1, is_ctx)).astype(BF16)
    col = 0
    for ref, width in ((pa_ref, IN_A), (pb_ref, IN_B), (pc_ref, IN_C), (pd_ref, IN_D)):
        ref[0] = _dot(h, w_ref[:, col:col + width]).astype(BF16)
        col += width


def _mod_specs(n_batch):
    width = N_MOD * D_MODEL
    return [pl.BlockSpec((1, 1, width), lambda b, j: (b, 0, 0)),
            pl.BlockSpec((1, 1, width), lambda b, j: (n_batch, 0, 0))]


def _in_proj(x, mod, g, w, n_lat):
    n_batch, s, d = x.shape
    widths = (IN_A, IN_B, IN_C, IN_D)
    return pl.pallas_call(
        functools.partial(_in_proj_kernel, n_lat=n_lat),
        out_shape=[jax.ShapeDtypeStruct((n_batch, s, wd), BF16) for wd in widths],
        grid=(n_batch, s // TOKEN_TILE),
        in_specs=[pl.BlockSpec((1, TOKEN_TILE, d), lambda b, j: (b, j, 0))] + _mod_specs(n_batch) + [
            pl.BlockSpec((1, d), lambda b, j: (0, 0)),
            pl.BlockSpec((d, D_IN), lambda b, j: (0, 0))],
        out_specs=[pl.BlockSpec((1, TOKEN_TILE, wd), lambda b, j: (b, j, 0)) for wd in widths],
        compiler_params=_cparams(("parallel", "arbitrary")),
        name="in_proj",
    )(x, mod, mod, g.reshape(1, d), w)


SCAN_SEGS = 8


LANE_CHUNKS = W_MIX // LANES


def _store_chunks(ref, value):
    rows = value.shape[0]
    for c in range(LANE_CHUNKS):
        ref[c * rows:(c + 1) * rows, :] = value[:, c * LANES:(c + 1) * LANES]


def _load_chunks(ref):
    rows = ref.shape[0] // LANE_CHUNKS
    return jnp.concatenate([ref[c * rows:(c + 1) * rows, :] for c in range(LANE_CHUNKS)], 1)


def _segmented_scan(a_ref, b_ref, p_ref, h_ref, row0, n, h0, reverse):
    seg = n // SCAN_SEGS
    chunk_rows = a_ref.shape[0] // LANE_CHUNKS

    def body(k, carry):
        kk = (seg - 1 - k) if reverse else k
        out = []
        for c in range(LANE_CHUNKS):
            h, p = carry[c]
            idx = pl.ds(c * chunk_rows + row0 + kk, SCAN_SEGS, stride=seg)
            a = a_ref[idx, :]
            h = a * h + b_ref[idx, :]
            p = a * p
            h_ref[idx, :] = h
            p_ref[idx, :] = p
            out.append((h, p))
        return tuple(out)

    init = tuple((jnp.zeros((SCAN_SEGS, LANES), F32), jnp.ones((SCAN_SEGS, LANES), F32))
                 for _ in range(LANE_CHUNKS))
    ends = lax.fori_loop(0, seg, body, init)
    final = []
    for c in range(LANE_CHUNKS):
        h_end, p_end = ends[c]
        carry = h0[c]
        carry_in = [None] * SCAN_SEGS
        for j in (range(SCAN_SEGS - 1, -1, -1) if reverse else range(SCAN_SEGS)):
            carry_in[j] = carry
            carry = h_end[j:j + 1] + p_end[j:j + 1] * carry
        for j in range(SCAN_SEGS):
            rows = pl.ds(c * chunk_rows + row0 + j * seg, seg)
            h_ref[rows, :] = h_ref[rows, :] + p_ref[rows, :] * carry_in[j]
        final.append(carry)
    return final


def _rglru_kernel(p_ref, cw_ref, cb_ref, wg_ref, bg_ref, lam_ref, o_ref,
                  xpad, a_s, b_s, p_s, hf_s, hb_s, *, n_lat, n_ctx):
    pad = 8
    lat0, ctx0 = pad, 2 * pad + n_lat
    zeros = jnp.zeros((pad, W_MIX), F32)
    xpad[0:pad, :] = zeros
    xpad[lat0:lat0 + n_lat, :] = p_ref[0, 0:n_lat, 0:W_MIX].astype(F32)
    xpad[lat0 + n_lat:ctx0, :] = zeros
    xpad[ctx0:ctx0 + n_ctx, :] = p_ref[0, n_lat:n_lat + n_ctx, 0:W_MIX].astype(F32)
    xpad[ctx0 + n_ctx:ctx0 + n_ctx + pad, :] = zeros

    for d, reverse in enumerate((False, True)):
        def conv(off, n):
            acc = cb_ref[d]
            for k in range(LRU_CONV):
                sh = k if reverse else k - (LRU_CONV - 1)
                acc = acc + cw_ref[d, k:k + 1, :] * xpad[off + sh:off + sh + n, :]
            return acc

        u = jnp.concatenate([conv(lat0, n_lat), conv(ctx0, n_ctx)], 0)
        gates = _dot(u.astype(BF16), wg_ref[d]) + bg_ref[d]
        r = _sigmoid(gates[:, 0:W_MIX])
        i = _sigmoid(gates[:, W_MIX:2 * W_MIX])
        z = -lam_ref[d]
        softplus = jnp.maximum(z, 0.0) + jnp.log(1.0 + jnp.exp(-jnp.abs(z)))
        a = jnp.exp(-RG_C * r * softplus)
        _store_chunks(a_s, a)
        _store_chunks(b_s, jnp.sqrt(1.0 - a * a) * (i * u))
        h_out = hb_s if reverse else hf_s
        zero_state = [jnp.zeros((1, LANES), F32)] * LANE_CHUNKS
        h_ctx = _segmented_scan(a_s, b_s, p_s, h_out, n_lat, n_ctx, zero_state, reverse)
        _segmented_scan(a_s, b_s, p_s, h_out, 0, n_lat, h_ctx, reverse)

    xg = p_ref[0, :, W_MIX:2 * W_MIX].astype(F32)
    o_ref[0] = ((_load_chunks(hf_s) + _load_chunks(hb_s)) * _gelu_tanh(xg)).astype(BF16)


def _rglru(p_a, conv_w, conv_b, w_gates, b_gates, lam, n_lat):
    n_batch, s, _ = p_a.shape
    n_ctx = s - n_lat
    full = lambda shape: pl.BlockSpec(shape, lambda b: (0,) * len(shape))
    return pl.pallas_call(
        functools.partial(_rglru_kernel, n_lat=n_lat, n_ctx=n_ctx),
        out_shape=jax.ShapeDtypeStruct((n_batch, s, W_MIX), BF16),
        grid=(n_batch,),
        in_specs=[pl.BlockSpec((1, s, IN_A), lambda b: (b, 0, 0)),
                  full((2, LRU_CONV, W_MIX)), full((2, 1, W_MIX)),
                  full((2, W_MIX, 2 * W_MIX)), full((2, 1, 2 * W_MIX)), full((2, 1, W_MIX))],
        out_specs=pl.BlockSpec((1, s, W_MIX), lambda b: (b, 0, 0)),
        scratch_shapes=[pltpu.VMEM((s + 24, W_MIX), F32)] + [pltpu.VMEM((LANE_CHUNKS * s, LANES), F32)] * 5,
        compiler_params=_cparams(("parallel",)),
        name="rglru",
    )(p_a, conv_w, conv_b, w_gates, b_gates, lam)


def _diff_attn_kernel(p_ref, cos_ref, sin_ref, lqk_ref, subg_ref, gmat_ref, o_ref, q_s, k_s,
                      *, n_lat, n_ctx, lam_init):
    j = pl.program_id(1)
    n_lat_tiles = n_lat // Q_TILE
    lane = lax.broadcasted_iota(jnp.int32, (1, W_MIX), 1)

    @pl.when(j == 0)
    def _():
        first_half = (lane % DIFF_DK) < (DIFF_DK // 2)

        def rope(x):
            partner = jnp.where(first_half, pltpu.roll(x, W_MIX - DIFF_DK // 2, 1),
                                pltpu.roll(x, DIFF_DK // 2, 1))
            return x * cos_ref[...] + partner * sin_ref[...]

        scale = DIFF_DK ** -0.5
        q = p_ref[0, :, 0:W_MIX].astype(F32)
        k = p_ref[0, :, W_MIX:2 * W_MIX].astype(F32)
        q_s[0:n_lat, :] = (rope(q[0:n_lat]) * scale).astype(BF16)
        q_s[n_lat:n_lat + n_ctx, :] = (q[n_lat:] * scale).astype(BF16)
        k_s[0:n_lat, :] = rope(k[0:n_lat]).astype(BF16)
        k_s[n_lat:n_lat + n_ctx, :] = k[n_lat:].astype(BF16)

    lqk = lqk_ref[...]
    lam = (jnp.exp(jnp.sum(lqk[0:1] * lqk[1:2], -1, keepdims=True))
           - jnp.exp(jnp.sum(lqk[2:3] * lqk[3:4], -1, keepdims=True)) + lam_init)

    def attend(kv0, nk):
        qt = q_s[pl.ds(pl.multiple_of(j * Q_TILE, Q_TILE), Q_TILE), :]
        keys = k_s[kv0:kv0 + nk, :]
        vals = p_ref[0, kv0:kv0 + nk, 2 * W_MIX:3 * W_MIX]
        acc = jnp.zeros((Q_TILE, W_MIX), F32)
        for h in range(DIFF_HEADS):
            o = []
            for c in range(2):
                qm = jnp.where(lane // DIFF_DK == 2 * h + c, qt, jnp.zeros_like(qt))
                s = lax.dot_general(qm, keys, (((1,), (1,)), ((), ())), preferred_element_type=F32)
                e = jnp.exp(s - jnp.max(s, -1, keepdims=True))
                o.append(_dot(e.astype(BF16), vals) / jnp.sum(e, -1, keepdims=True))
            acc = jnp.where(lane // DIFF_DV == h, o[0] - lam * o[1], acc)
        ms = _dot2_lhs(acc * acc, gmat_ref[...])
        y = acc * lax.rsqrt(ms + EPS) * subg_ref[...] * (1.0 - lam_init)
        o_ref[0] = y.astype(BF16)

    @pl.when(j < n_lat_tiles)
    def _():
        attend(0, n_lat + n_ctx)

    @pl.when(j >= n_lat_tiles)
    def _():
        attend(n_lat, n_ctx)


def _diff_attn(p_b, rope_cos, rope_sin, lqk, sub_g, gmat, n_lat, lam_init):
    n_batch, s, _ = p_b.shape
    n_ctx = s - n_lat
    full = lambda shape: pl.BlockSpec(shape, lambda b, j: (0,) * len(shape))
    return pl.pallas_call(
        functools.partial(_diff_attn_kernel, n_lat=n_lat, n_ctx=n_ctx, lam_init=lam_init),
        out_shape=jax.ShapeDtypeStruct((n_batch, s, W_MIX), BF16),
        grid=(n_batch, s // Q_TILE),
        in_specs=[pl.BlockSpec((1, s, IN_B), lambda b, j: (b, 0, 0)),
                  full((n_lat, W_MIX)), full((n_lat, W_MIX)), full((4, DIFF_DK)),
                  full((1, W_MIX)), full((W_MIX, W_MIX))],
        out_specs=pl.BlockSpec((1, Q_TILE, W_MIX), lambda b, j: (b, j, 0)),
        scratch_shapes=[pltpu.VMEM((s, W_MIX), BF16), pltpu.VMEM((s, W_MIX), BF16)],
        compiler_params=_cparams(("parallel", "arbitrary")),
        name="diff_attn",
    )(p_b, rope_cos, rope_sin, lqk, sub_g, gmat)


def _conformer_kernel(p_ref, cw_ref, cb_ref, lng_ref, lnb_ref, wpw_ref, bpw_ref, o_ref, upad,
                      *, n_lat, n_ctx):
    pad = 16
    half = CONF_K // 2
    lat0, ctx0 = pad, 2 * pad + n_lat
    val = p_ref[0, :, 0:W_MIX].astype(F32)
    gate = p_ref[0, :, W_MIX:2 * W_MIX].astype(F32)
    u = val * _sigmoid(gate)
    zeros = jnp.zeros((pad, W_MIX), F32)
    upad[0:pad, :] = zeros
    upad[lat0:lat0 + n_lat, :] = u[0:n_lat]
    upad[lat0 + n_lat:ctx0, :] = zeros
    upad[ctx0:ctx0 + n_ctx, :] = u[n_lat:]
    upad[ctx0 + n_ctx:ctx0 + n_ctx + pad, :] = zeros

    def conv(off, n):
        acc = cb_ref[...]
        for k in range(CONF_K):
            acc = acc + cw_ref[k:k + 1, :] * upad[off + k - half:off + k - half + n, :]
        return acc

    y = jnp.concatenate([conv(lat0, n_lat), conv(ctx0, n_ctx)], 0)
    mu = jnp.mean(y, -1, keepdims=True)
    yc = y - mu
    var = jnp.mean(yc * yc, -1, keepdims=True)
    y = yc * lax.rsqrt(var + 1e-5) * lng_ref[...] + lnb_ref[...]
    o_ref[0] = (_dot(_silu(y).astype(BF16), wpw_ref[...]) + bpw_ref[...]).astype(BF16)


def _conformer(p_c, conv_w, conv_b, ln_g, ln_b, w_pw, b_pw, n_lat):
    n_batch, s, _ = p_c.shape
    full = lambda shape: pl.BlockSpec(shape, lambda b: (0,) * len(shape))
    return pl.pallas_call(
        functools.partial(_conformer_kernel, n_lat=n_lat, n_ctx=s - n_lat),
        out_shape=jax.ShapeDtypeStruct((n_batch, s, W_MIX), BF16),
        grid=(n_batch,),
        in_specs=[pl.BlockSpec((1, s, IN_C), lambda b: (b, 0, 0)),
                  full((CONF_K, W_MIX)), full((1, W_MIX)), full((1, W_MIX)), full((1, W_MIX)),
                  full((W_MIX, W_MIX)), full((1, W_MIX))],
        out_specs=pl.BlockSpec((1, s, W_MIX), lambda b: (b, 0, 0)),
        scratch_shapes=[pltpu.VMEM((s + 48, W_MIX), F32)],
        compiler_params=_cparams(("parallel",)),
        name="conformer",
    )(p_c, conv_w, conv_b, ln_g, ln_b, w_pw, b_pw)


def _dft_matrices():
    n, hb = CONV_FFT, CONV_BLOCK
    k = np.arange(hb, dtype=np.float64)[:, None]
    t = np.arange(n, dtype=np.float64)[None, :]
    ang = 2.0 * np.pi * k * t / n
    full = np.concatenate([np.cos(ang), -np.sin(ang)], 0)
    full[hb, :] = np.cos(np.pi * t[0])
    fwd = full[:, :hb]
    wk = np.where(k == 0, 1.0, 2.0) / n
    tt = np.arange(hb, dtype=np.float64)[None, :]
    ang_i = 2.0 * np.pi * k * tt / n
    inv = np.concatenate([wk * np.cos(ang_i), -wk * np.sin(ang_i)], 0)
    inv[hb, :] = np.cos(np.pi * tt[0]) / n
    sign = np.cos(np.pi * k)
    filt = np.concatenate([sign * np.cos(ang), -sign * np.sin(ang)], 0)
    filt[hb, :] = np.cos(np.pi * t[0])
    filt[:, 0] = 0.0
    return fwd.astype(np.float32), inv.T.copy().astype(np.float32), filt.astype(np.float32)


def _hyena_positions(n_rows, center, seq_len):
    lag = np.arange(n_rows, dtype=np.float64) - center
    t = np.abs(lag)
    valid = t < seq_len
    t_unit = t / max(seq_len - 1, 1)
    bands = np.linspace(1e-4, HYENA_BANDS - 1, HYENA_BANDS)
    ang = (2.0 * np.pi / seq_len) * t[:, None] * bands[None, :]
    feats = np.zeros((n_rows, LANES), np.float64)
    feats[:, 0] = t_unit
    feats[:, 1:1 + HYENA_BANDS] = np.cos(ang)
    feats[:, 1 + HYENA_BANDS:HYENA_EMB] = -np.sin(ang)
    aux = np.zeros((n_rows, LANES), np.float64)
    aux[:, 0] = t_unit
    aux[:, 1] = valid
    aux[:, 2] = lag >= 0
    return feats.astype(np.float32), aux.astype(np.float32)


def _hyena_filter_kernel(fl_ref, al_ref, fc_ref, ac_ref, w1_ref, b1_ref, fr_ref, w2_ref, b2_ref,
                         w3_ref, dec_ref, filt_ref, a_lat, b_lat, n_lat, a_ctx, b_ctx, n_ctx):
    filt = filt_ref[...]
    row0 = lax.broadcasted_iota(jnp.int32, (CONV_BLOCK, 1), 0) == 0

    def taps(feats_ref, aux_ref):
        f = jnp.sin(fr_ref[0] * (_dot3(feats_ref[...], w1_ref[0]) + b1_ref[0]))
        f = jnp.sin(fr_ref[0] * (_dot3(f, w2_ref[0]) + b2_ref[0]))
        t_unit = aux_ref[:, 0:1]
        h = _dot3(f, w3_ref[0, 0]) * jnp.exp(-t_unit * jnp.abs(dec_ref[0, 0]))
        g = jnp.where(aux_ref[:, 2:3] > 0.5, h[:, 0:W_MIX], h[:, W_MIX:2 * W_MIX]) * aux_ref[:, 1:2]
        energy = jnp.sum(g * g, 0, keepdims=True)
        return g * lax.rsqrt(energy + EPS)

    def spectrum(window, a_out, b_out, n_out):
        hp = _dot2_rhs(filt, window)
        a_out[...] = hp[0:CONV_BLOCK]
        b_out[...] = jnp.where(row0, 0.0, hp[CONV_BLOCK:])
        n_out[...] = hp[CONV_BLOCK:CONV_BLOCK + 1]

    g = taps(fl_ref, al_ref)
    for dd in range(a_lat.shape[2]):
        spectrum(g[dd * CONV_BLOCK:dd * CONV_BLOCK + CONV_FFT],
                 a_lat.at[0, 0, dd], b_lat.at[0, 0, dd], n_lat.at[0, 0, dd])
    spectrum(taps(fc_ref, ac_ref), a_ctx.at[0, 0], b_ctx.at[0, 0], n_ctx.at[0, 0])


def _hyena_filters(feats_l, aux_l, feats_c, aux_c, w1, b1, freq, w2, b2, w3, decay, filt_m):
    depth = w1.shape[0]
    rows_l = feats_l.shape[0]
    n_off = rows_l // CONV_BLOCK - 1
    full = lambda shape: pl.BlockSpec(shape, lambda l, o: (0,) * len(shape))
    per_layer = lambda shape: pl.BlockSpec((1,) + shape, lambda l, o: (l,) + (0,) * len(shape))
    spec5 = lambda r: pl.BlockSpec((1, 1, n_off, r, W_MIX), lambda l, o: (l, o, 0, 0, 0))
    spec4 = lambda r: pl.BlockSpec((1, 1, r, W_MIX), lambda l, o: (l, o, 0, 0))
    sds = jax.ShapeDtypeStruct
    return pl.pallas_call(
        _hyena_filter_kernel,
        out_shape=[sds((depth, HYENA_ORDER, n_off, CONV_BLOCK, W_MIX), F32),
                   sds((depth, HYENA_ORDER, n_off, CONV_BLOCK, W_MIX), F32),
                   sds((depth, HYENA_ORDER, n_off, 1, W_MIX), F32),
                   sds((depth, HYENA_ORDER, CONV_BLOCK, W_MIX), F32),
                   sds((depth, HYENA_ORDER, CONV_BLOCK, W_MIX), F32),
                   sds((depth, HYENA_ORDER, 1, W_MIX), F32)],
        grid=(depth, HYENA_ORDER),
        in_specs=[full(feats_l.shape), full(aux_l.shape), full(feats_c.shape), full(aux_c.shape),
                  per_layer((LANES, HYENA_FFN)), per_layer((1, HYENA_FFN)), per_layer((1, HYENA_FFN)),
                  per_layer((HYENA_FFN, HYENA_FFN)), per_layer((1, HYENA_FFN)),
                  pl.BlockSpec((1, 1, HYENA_FFN, 2 * W_MIX), lambda l, o: (l, o, 0, 0)),
                  pl.BlockSpec((1, 1, 1, 2 * W_MIX), lambda l, o: (l, o, 0, 0)),
                  full(filt_m.shape)],
        out_specs=[spec5(CONV_BLOCK), spec5(CONV_BLOCK), spec5(1),
                   spec4(CONV_BLOCK), spec4(CONV_BLOCK), spec4(1)],
        compiler_params=_cparams(("arbitrary", "arbitrary")),
        name="hyena_filters",
    )(feats_l, aux_l, feats_c, aux_c, w1, b1, freq, w2, b2, w3, decay, filt_m)


def _hyena_kernel(p_ref, cw_ref, cb_ref, bias_ref, al_ref, bl_ref, nl_ref, ac_ref, bc_ref, nc_ref,
                  fwd_ref, inv_ref, o_ref, upad, *, n_lat, n_ctx):
    pad = 8
    lat0, ctx0 = pad, 2 * pad + n_lat
    n_blk = n_lat // CONV_BLOCK
    zeros = jnp.zeros((pad, IN_D), F32)
    upad[0:pad, :] = zeros
    upad[lat0:lat0 + n_lat, :] = p_ref[0, 0:n_lat, :].astype(F32)
    upad[lat0 + n_lat:ctx0, :] = zeros
    upad[ctx0:ctx0 + n_ctx, :] = p_ref[0, n_lat:n_lat + n_ctx, :].astype(F32)
    upad[ctx0 + n_ctx:ctx0 + n_ctx + pad, :] = zeros

    def conv(off, n):
        acc = cb_ref[...]
        for k in range(HYENA_SHORT):
            sh = k - HYENA_SHORT // 2
            acc = acc + cw_ref[k:k + 1, :] * upad[off + sh:off + sh + n, :]
        return acc

    u = jnp.concatenate([conv(lat0, n_lat), conv(ctx0, n_ctx)], 0)
    row0 = lax.broadcasted_iota(jnp.int32, (CONV_BLOCK, 1), 0) == 0
    fwd = fwd_ref[...]
    inv = inv_ref[...]

    def block_spectrum(zb):
        zf = _dot(fwd, zb.astype(BF16))
        return zf[0:CONV_BLOCK], zf[CONV_BLOCK:]

    def block_output(pre, pim, pnyq):
        pim = jnp.where(row0, pnyq, pim)
        return _dot(inv, jnp.concatenate([pre, pim], 0).astype(BF16))

    z = u[:, 0:W_MIX]
    for o in range(HYENA_ORDER):
        gate = u[:, (o + 1) * W_MIX:(o + 2) * W_MIX]
        spec = [block_spectrum(z[jb * CONV_BLOCK:(jb + 1) * CONV_BLOCK]) for jb in range(n_blk)]
        outs = []
        for ib in range(n_blk):
            pre = pim = pnyq = None
            for jb in range(n_blk):
                dd = ib - jb + n_blk - 1
                a, b, nq = al_ref[o, dd], bl_ref[o, dd], nl_ref[o, dd]
                zre, zim = spec[jb]
                t_re = zre * a - zim * b
                t_im = zre * b + zim * a
                t_ny = zim[0:1] * nq
                pre = t_re if pre is None else pre + t_re
                pim = t_im if pim is None else pim + t_im
                pnyq = t_ny if pnyq is None else pnyq + t_ny
            outs.append(block_output(pre, pim, pnyq))
        zc = jnp.concatenate([z[n_lat:], jnp.zeros((CONV_BLOCK - n_ctx, W_MIX), F32)], 0)
        zre, zim = block_spectrum(zc)
        a, b, nq = ac_ref[o], bc_ref[o], nc_ref[o]
        outs.append(block_output(zre * a - zim * b, zre * b + zim * a, zim[0:1] * nq)[0:n_ctx])
        y = jnp.concatenate(outs, 0)
        z = gate * (y + bias_ref[o] * z)
    o_ref[0] = z.astype(BF16)


def _hyena(p_d, conv_w, conv_b, bias, filters, fwd_m, inv_m, n_lat):
    n_batch, s, _ = p_d.shape
    n_off = 2 * (n_lat // CONV_BLOCK) - 1
    full = lambda shape: pl.BlockSpec(shape, lambda b: (0,) * len(shape))
    a_lat, b_lat, n_lat_f, a_ctx, b_ctx, n_ctx_f = filters
    return pl.pallas_call(
        functools.partial(_hyena_kernel, n_lat=n_lat, n_ctx=s - n_lat),
        out_shape=jax.ShapeDtypeStruct((n_batch, s, W_MIX), BF16),
        grid=(n_batch,),
        in_specs=[pl.BlockSpec((1, s, IN_D), lambda b: (b, 0, 0)),
                  full((HYENA_SHORT, IN_D)), full((1, IN_D)), full((HYENA_ORDER, 1, W_MIX)),
                  full((HYENA_ORDER, n_off, CONV_BLOCK, W_MIX)), full((HYENA_ORDER, n_off, CONV_BLOCK, W_MIX)),
                  full((HYENA_ORDER, n_off, 1, W_MIX)),
                  full((HYENA_ORDER, CONV_BLOCK, W_MIX)), full((HYENA_ORDER, CONV_BLOCK, W_MIX)),
                  full((HYENA_ORDER, 1, W_MIX)),
                  full((CONV_FFT, CONV_BLOCK)), full((CONV_BLOCK, CONV_FFT))],
        out_specs=pl.BlockSpec((1, s, W_MIX), lambda b: (b, 0, 0)),
        scratch_shapes=[pltpu.VMEM((s + 24, IN_D), F32)],
        compiler_params=_cparams(("parallel",)),
        name="hyena",
    )(p_d, conv_w, conv_b, bias, a_lat, b_lat, n_lat_f, a_ctx, b_ctx, n_ctx_f, fwd_m, inv_m)


ROUTE_E0, ROUTE_E1, ROUTE_R0, ROUTE_R1, ROUTE_G0, ROUTE_G1 = range(6)


def _out_route_kernel(x_ref, ya_ref, yb_ref, yc_ref, yd_ref, mb_ref, mc_ref, g_ref, wo_ref, wr_ref,
                      br_ref, tri_ref, xo_ref, h_ref, route_ref, cnt_ref, *, n_lat, route_ctx):
    is_ctx = _is_ctx_rows(pl.program_id(1), TOKEN_TILE, n_lat)
    mix = None
    for g, ref in enumerate((ya_ref, yb_ref, yc_ref, yd_ref)):
        part = _dot(ref[0], wo_ref[g * W_MIX:(g + 1) * W_MIX, :])
        mix = part if mix is None else mix + part
    x = x_ref[0] + _mod_vec(mb_ref, mc_ref, 2, is_ctx) * mix
    xo_ref[0] = x
    h = _rms_modulate(x, g_ref[...], _mod_vec(mb_ref, mc_ref, 3, is_ctx), _mod_vec(mb_ref, mc_ref, 4, is_ctx))
    h_ref[0] = h

    logits = _dot3(h, wr_ref[...]) + br_ref[...]
    lane = lax.broadcasted_iota(jnp.int32, (1, LANES), 1)
    neg = jnp.float32(-jnp.inf)

    def top1(vals):
        m = jnp.max(vals, -1, keepdims=True)
        idx = jnp.min(jnp.where(vals == m, lane, LANES), -1, keepdims=True)
        return m, idx

    grp = jnp.where(lane < N_GROUPS, logits, neg)
    m_g, g_idx = top1(grp)
    p_grp = 1.0 / jnp.sum(jnp.exp(grp - m_g), -1, keepdims=True)
    lo = N_GROUPS + g_idx * EXPERTS_PER_GROUP
    exp_l = jnp.where((lane >= lo) & (lane < lo + EXPERTS_PER_GROUP), logits, neg)
    m1, i1 = top1(exp_l)
    m2, i2 = top1(jnp.where(lane == i1, neg, exp_l))
    e2 = jnp.exp(m2 - m1)
    gate0 = p_grp / (1.0 + e2)
    gate1 = p_grp * e2 / (1.0 + e2)

    live = jnp.ones_like(is_ctx) if route_ctx else jnp.logical_not(is_ctx)
    oh0 = jnp.where((lane == i1) & live, 1.0, 0.0)
    oh1 = jnp.where((lane == i2) & live, 1.0, 0.0)
    both = jnp.concatenate([oh0, oh1], 1).astype(BF16)
    before = _dot(tri_ref[...], both)
    cnt0 = jnp.sum(oh0, 0, keepdims=True)
    rank0 = jnp.sum(before[:, 0:LANES] * oh0, -1, keepdims=True)
    rank1 = jnp.sum((before[:, LANES:] + cnt0) * oh1, -1, keepdims=True)
    cnt_ref[0] = (cnt0 + jnp.sum(oh1, 0, keepdims=True)).astype(jnp.int32)

    slab = jnp.zeros((TOKEN_TILE, LANES), F32)
    for col, v in ((ROUTE_E0, (i1 - N_GROUPS).astype(F32)), (ROUTE_E1, (i2 - N_GROUPS).astype(F32)),
                   (ROUTE_R0, rank0), (ROUTE_R1, rank1), (ROUTE_G0, gate0), (ROUTE_G1, gate1)):
        slab = jnp.where(lane == col, v, slab)
    route_ref[0] = slab


def _out_route(x, ys, mod, g, w_out, w_route, b_route, tri, n_lat, route_ctx):
    n_batch, s, d = x.shape
    tiles = s // TOKEN_TILE
    tok = lambda width: pl.BlockSpec((1, TOKEN_TILE, width), lambda b, j: (b, j, 0))
    full = lambda shape: pl.BlockSpec(shape, lambda b, j: (0,) * len(shape))
    sds = jax.ShapeDtypeStruct
    return pl.pallas_call(
        functools.partial(_out_route_kernel, n_lat=n_lat, route_ctx=route_ctx),
        out_shape=[sds((n_batch, s, d), F32), sds((n_batch, s, d), F32), sds((n_batch, s, LANES), F32),
                   sds((n_batch * tiles, 1, LANES), jnp.int32)],
        grid=(n_batch, tiles),
        in_specs=[tok(d)] + [tok(W_MIX)] * 4 + _mod_specs(n_batch) + [
            full((1, d)), full((4 * W_MIX, d)), full((d, LANES)), full((1, LANES)),
            full((TOKEN_TILE, TOKEN_TILE))],
        out_specs=[tok(d), tok(d), tok(LANES),
                   pl.BlockSpec((1, 1, LANES), lambda b, j: (b * tiles + j, 0, 0))],
        input_output_aliases={0: 0},
        compiler_params=_cparams(("parallel", "arbitrary")),
        name="out_route",
    )(x, *ys, mod, mod, g.reshape(1, d), w_out, w_route, b_route, tri)


def _row_copy(src, src_row, dst, dst_row, sem):
    return pltpu.make_async_copy(src.at[pl.ds(src_row, 1), :], dst.at[pl.ds(dst_row, 1), :], sem)


def _dispatch_kernel(dest_ref, h_ref, xs_in_ref, xs_ref, sem):
    del xs_in_ref

    def start(r, c):
        for k in range(2):
            d = dest_ref[0, 0, 2 * r + k]

            @pl.when(d >= 0)
            def _():
                _row_copy(h_ref, r, xs_ref, d, sem).start()
        return c

    def wait(r, c):
        for k in range(2):
            d = dest_ref[0, 0, 2 * r + k]

            @pl.when(d >= 0)
            def _():
                _row_copy(h_ref, r, xs_ref, d, sem).wait()
        return c

    lax.fori_loop(0, TOKEN_TILE, start, 0)
    lax.fori_loop(0, TOKEN_TILE, wait, 0)


def _dispatch(h_tok, dest, n_slots):
    n_tok, d = h_tok.shape
    tiles = n_tok // TOKEN_TILE
    return pl.pallas_call(
        _dispatch_kernel,
        out_shape=jax.ShapeDtypeStruct((n_slots, d), F32),
        grid=(tiles,),
        in_specs=[pl.BlockSpec((1, 1, 2 * TOKEN_TILE), lambda i: (i, 0, 0), memory_space=pltpu.SMEM),
                  pl.BlockSpec((TOKEN_TILE, d), lambda i: (i, 0)),
                  pl.BlockSpec(memory_space=pl.ANY)],
        out_specs=pl.BlockSpec(memory_space=pl.ANY),
        scratch_shapes=[pltpu.SemaphoreType.DMA(())],
        input_output_aliases={2: 0},
        compiler_params=_cparams(("arbitrary",)),
        name="moe_dispatch",
    )(dest.reshape(tiles, 1, 2 * TOKEN_TILE), h_tok, jnp.zeros((n_slots, d), F32))


def _expert_kernel(be_ref, nb_ref, x_ref, wgu_ref, wd_ref, y_ref):
    i = pl.program_id(0)

    @pl.when(i < nb_ref[0])
    def _():
        x = x_ref[...].astype(BF16)
        gu = _dot(x, wgu_ref[0])
        hid = _silu(gu[:, 0:D_EXPERT]) * gu[:, D_EXPERT:]
        y_ref[...] = _dot(hid.astype(BF16), wd_ref[0])

    @pl.when(i >= nb_ref[0])
    def _():
        y_ref[...] = jnp.zeros_like(y_ref)


def _experts(blk_expert, n_used, xs, w_gu, w_down):
    n_slots, d = xs.shape
    return pl.pallas_call(
        _expert_kernel,
        out_shape=jax.ShapeDtypeStruct((n_slots, d), F32),
        grid_spec=pltpu.PrefetchScalarGridSpec(
            num_scalar_prefetch=2,
            grid=(n_slots // EXPERT_TILE,),
            in_specs=[pl.BlockSpec((EXPERT_TILE, d), lambda i, be, nb: (i, 0)),
                      pl.BlockSpec((1, d, 2 * D_EXPERT), lambda i, be, nb: (be[i], 0, 0)),
                      pl.BlockSpec((1, D_EXPERT, d), lambda i, be, nb: (be[i], 0, 0))],
            out_specs=pl.BlockSpec((EXPERT_TILE, d), lambda i, be, nb: (i, 0))),
        compiler_params=_cparams(("arbitrary",)),
        name="moe_experts",
    )(blk_expert, n_used, xs, w_gu, w_down)


def _combine_kernel(dest_ref, x_ref, route_ref, mb_ref, mc_ref, ys_ref, xo_ref, buf0, buf1, sem, *, n_lat):
    bufs = (buf0, buf1)

    def copies(r):
        return [_row_copy(ys_ref, jnp.maximum(dest_ref[0, 0, 2 * r + k], 0), bufs[k], r, sem.at[k])
                for k in range(2)]

    def start(r, c):
        for cp in copies(r):
            cp.start()
        return c

    def wait(r, c):
        for cp in copies(r):
            cp.wait()
        return c

    lax.fori_loop(0, TOKEN_TILE, start, 0)
    lax.fori_loop(0, TOKEN_TILE, wait, 0)
    is_ctx = _is_ctx_rows(pl.program_id(1), TOKEN_TILE, n_lat)
    route = route_ref[0]
    moe = route[:, ROUTE_G0:ROUTE_G0 + 1] * buf0[...] + route[:, ROUTE_G1:ROUTE_G1 + 1] * buf1[...]
    xo_ref[0] = x_ref[0] + _mod_vec(mb_ref, mc_ref, 5, is_ctx) * moe


def _combine(x, route, mod, dest, ys, n_lat):
    n_batch, s, d = x.shape
    tiles = s // TOKEN_TILE
    tok = lambda width: pl.BlockSpec((1, TOKEN_TILE, width), lambda b, j: (b, j, 0))
    return pl.pallas_call(
        functools.partial(_combine_kernel, n_lat=n_lat),
        out_shape=jax.ShapeDtypeStruct((n_batch, s, d), F32),
        grid=(n_batch, tiles),
        in_specs=[pl.BlockSpec((1, 1, 2 * TOKEN_TILE), lambda b, j: (b * tiles + j, 0, 0),
                               memory_space=pltpu.SMEM),
                  tok(d), tok(LANES)] + _mod_specs(n_batch) + [pl.BlockSpec(memory_space=pl.ANY)],
        out_specs=tok(d),
        scratch_shapes=[pltpu.VMEM((TOKEN_TILE, d), F32), pltpu.VMEM((TOKEN_TILE, d), F32),
                        pltpu.SemaphoreType.DMA((2,))],
        input_output_aliases={1: 0},
        compiler_params=_cparams(("arbitrary", "arbitrary")),
        name="moe_combine",
    )(dest.reshape(n_batch * tiles, 1, 2 * TOKEN_TILE), x, route, mod, mod, ys)


def _moe_slots(route, counts, n_experts_pad):
    n_batch, s, _ = route.shape
    tiles = counts.shape[0]
    cnt = counts[:, 0, N_GROUPS:N_GROUPS + N_EXPERTS]
    total = jnp.sum(cnt, 0)
    padded = (total + EXPERT_TILE - 1) // EXPERT_TILE * EXPERT_TILE
    p_end = jnp.cumsum(padded)
    base = (p_end - padded)[None, :] + jnp.cumsum(cnt, 0) - cnt
    e = route[..., ROUTE_E0:ROUTE_E1 + 1].astype(jnp.int32).reshape(tiles, TOKEN_TILE, 2)
    rank = route[..., ROUTE_R0:ROUTE_R1 + 1].astype(jnp.int32).reshape(tiles, TOKEN_TILE, 2)
    dest = jnp.take_along_axis(base, e.reshape(tiles, -1), axis=1).reshape(e.shape) + rank
    n_blk = n_experts_pad // EXPERT_TILE
    blk_expert = jnp.minimum(
        jnp.searchsorted(p_end, jnp.arange(n_blk, dtype=jnp.int32) * EXPERT_TILE, side="right"),
        N_EXPERTS - 1).astype(jnp.int32)
    n_used = (p_end[-1] // EXPERT_TILE).astype(jnp.int32).reshape(1)
    return dest.astype(jnp.int32), blk_expert, n_used


def _final_kernel(x_ref, g_ref, o_ref):
    x = x_ref[0]
    o_ref[0] = x * lax.rsqrt(jnp.mean(x * x, -1, keepdims=True) + EPS) * g_ref[...]


def _final_norm(x, g, n_lat):
    n_batch, _, d = x.shape
    tm = 512
    return pl.pallas_call(
        _final_kernel,
        out_shape=jax.ShapeDtypeStruct((n_batch, n_lat, d), F32),
        grid=(n_batch, n_lat // tm),
        in_specs=[pl.BlockSpec((1, tm, d), lambda b, j: (b, j, 0)), pl.BlockSpec((1, d), lambda b, j: (0, 0))],
        out_specs=pl.BlockSpec((1, tm, d), lambda b, j: (b, j, 0)),
        compiler_params=_cparams(("parallel", "arbitrary")),
        name="final_norm",
    )(x, g.reshape(1, d))


def _block_diag(w):
    heads, di, dj = w.shape
    eye = jnp.eye(heads, dtype=w.dtype)
    return (eye[:, None, :, None] * w[:, :, None, :]).reshape(heads * di, heads * dj)


def _rope_tables(n_lat):
    rows = n_lat // GRID_W
    row = np.repeat(np.arange(rows, dtype=np.float64), GRID_W)
    col = np.tile(np.arange(GRID_W, dtype=np.float64), rows)
    inv = ROPE_BASE ** (-np.arange(ROPE_FREQS, dtype=np.float64) / ROPE_FREQS)
    ang = np.concatenate([row[:, None] * inv, col[:, None] * inv], -1)
    cos = np.tile(np.concatenate([np.cos(ang), np.cos(ang)], -1), (1, W_MIX // DIFF_DK))
    sin = np.tile(np.concatenate([-np.sin(ang), np.sin(ang)], -1), (1, W_MIX // DIFF_DK))
    return jnp.asarray(cos, F32), jnp.asarray(sin, F32)


def kernel(x, c, ctx, c_ctx, w_ada, b_ada, g_mix, g_ffn, w_in, w_out, a_conv_w, a_conv_b, a_w_r, a_b_r,
           a_w_i, a_b_i, a_lam, b_lq1, b_lk1, b_lq2, b_lk2, b_sub_g, c_conv_w, c_conv_b, c_ln_g, c_ln_b,
           c_w_pw, c_b_pw, d_conv_w, d_conv_b, d_w_f1, d_b_f1, d_freq, d_w_f2, d_b_f2, d_w_f3, d_decay,
           d_bias, moe_w_rg, moe_b_rg, moe_w_re, moe_b_re, moe_w_gate, moe_w_up, moe_w_down, g_final):
    n_batch, n_lat, d = x.shape
    n_ctx = ctx.shape[1]
    depth = w_ada.shape[0]
    s = n_lat + n_ctx
    assert d == D_MODEL and s % TOKEN_TILE == 0 and n_lat % CONV_BLOCK == 0 and n_ctx <= CONV_BLOCK
    assert n_lat % Q_TILE == 0 and n_ctx % Q_TILE == 0 and n_lat % GRID_W == 0

    xs = jnp.concatenate([x, ctx], 1)
    mod_rows = -(-(n_batch + 1) // MOD_ROWS_PAD) * MOD_ROWS_PAD
    c_all = jnp.concatenate([c, c_ctx[None], jnp.zeros((mod_rows - n_batch - 1, d), F32)], 0)
    mod_all = _ada_table(c_all, w_ada, b_ada)

    rope_cos, rope_sin = _rope_tables(n_lat)
    fwd_np, inv_np, filt_np = _dft_matrices()
    fwd_m, inv_m, filt_m = (jnp.asarray(m, F32).astype(BF16) for m in (fwd_np, inv_np, filt_np))
    feats_l, aux_l = (jnp.asarray(m) for m in _hyena_positions(2 * n_lat, n_lat, n_lat))
    feats_c, aux_c = (jnp.asarray(m) for m in _hyena_positions(CONV_FFT, CONV_BLOCK, n_ctx))
    gmat = jnp.asarray(np.kron(np.eye(DIFF_HEADS), np.full((DIFF_DV, DIFF_DV), 1.0 / DIFF_DV)), F32).astype(BF16)
    tri = jnp.asarray(np.tril(np.ones((TOKEN_TILE, TOKEN_TILE)), -1), F32).astype(BF16)

    w1 = jnp.pad(d_w_f1, ((0, 0), (0, LANES - HYENA_EMB), (0, 0)))
    w3 = d_w_f3.reshape(depth, HYENA_FFN, HYENA_ORDER, 2 * W_MIX).transpose(0, 2, 1, 3)
    dec = d_decay.reshape(depth, HYENA_ORDER, 1, 2 * W_MIX)
    row = lambda v: v.reshape(depth, 1, -1)
    hy = _hyena_filters(feats_l, aux_l, feats_c, aux_c, w1, row(d_b_f1), row(d_freq), d_w_f2, row(d_b_f2),
                        w3, dec, filt_m)

    n_tok = n_batch * s
    for l in range(depth):
        last = l == depth - 1
        lam_init = 0.8 - 0.6 * math.exp(-0.3 * l)
        mod = mod_all[l].reshape(mod_rows, 1, N_MOD * d)
        p_a, p_b, p_c, p_d = _in_proj(xs, mod, g_mix[l], w_in[l].astype(BF16), n_lat)

        w_gates = jnp.stack([jnp.concatenate([_block_diag(a_w_r[l, dr]), _block_diag(a_w_i[l, dr])], 1)
                             for dr in range(2)]).astype(BF16)
        b_gates = jnp.concatenate([a_b_r[l], a_b_i[l]], -1)[:, None, :]
        y_a = _rglru(p_a, a_conv_w[l], a_conv_b[l][:, None, :], w_gates, b_gates, a_lam[l][:, None, :], n_lat)
        lqk = jnp.stack([b_lq1[l], b_lk1[l], b_lq2[l], b_lk2[l]])
        y_b = _diff_attn(p_b, rope_cos, rope_sin, lqk, jnp.tile(b_sub_g[l], DIFF_HEADS)[None], gmat,
                         n_lat, lam_init)
        y_c = _conformer(p_c, c_conv_w[l], c_conv_b[l][None], c_ln_g[l][None], c_ln_b[l][None],
                         c_w_pw[l].astype(BF16), c_b_pw[l][None], n_lat)
        y_d = _hyena(p_d, d_conv_w[l], d_conv_b[l][None], d_bias[l][:, None, :], [f[l] for f in hy],
                     fwd_m, inv_m, n_lat)

        w_route = jnp.pad(jnp.concatenate([moe_w_rg[l], moe_w_re[l]], 1),
                          ((0, 0), (0, LANES - N_GROUPS - N_EXPERTS)))
        b_route = jnp.pad(jnp.concatenate([moe_b_rg[l], moe_b_re[l]]), (0, LANES - N_GROUPS - N_EXPERTS))[None]
        xs, h, route, counts = _out_route(xs, (y_a, y_b, y_c, y_d), mod, g_ffn[l], w_out[l].astype(BF16),
                                          w_route, b_route, tri, n_lat, not last)

        n_assign = 2 * (n_tok if not last else n_batch * n_lat)
        n_slots = -(-n_assign // EXPERT_TILE) * EXPERT_TILE + N_EXPERTS * EXPERT_TILE
        dest, blk_expert, n_used = _moe_slots(route, counts, n_slots)
        if last:
            live = (jnp.arange(n_tok, dtype=jnp.int32) % s < n_lat).reshape(dest.shape[0], TOKEN_TILE, 1)
            dest = jnp.where(live, dest, -1)
        slots = _dispatch(h.reshape(n_tok, d), dest, n_slots)
        w_gu = jnp.concatenate([moe_w_gate[l], moe_w_up[l]], -1).astype(BF16)
        ys = _experts(blk_expert, n_used, slots, w_gu, moe_w_down[l].astype(BF16))
        xs = _combine(xs, route, mod, dest, ys, n_lat)

    return _final_norm(xs, g_final, n_lat)
```

```python
import functools
import math

import numpy as np
import jax
import jax.numpy as jnp
from jax import lax
from jax.experimental import pallas as pl
from jax.experimental.pallas import tpu as pltpu

F32 = jnp.float32
BF16 = jnp.bfloat16

D_MODEL = 1024
N_MOD = 6
EPS = 1e-6
W_MIX = 256
LRU_HEADS = 4
LRU_CONV = 4
RG_C = 8.0
DIFF_HEADS = 4
DIFF_DV = W_MIX // DIFF_HEADS
DIFF_DK = DIFF_DV // 2
ROPE_BASE = 10000.0
ROPE_FREQS = DIFF_DK // 4
GRID_W = 64
CONF_K = 31
HYENA_ORDER = 2
HYENA_SHORT = 3
HYENA_EMB = 33
HYENA_BANDS = (HYENA_EMB - 1) // 2
HYENA_FFN = 64
N_GROUPS = 4
EXPERTS_PER_GROUP = 8
N_EXPERTS = N_GROUPS * EXPERTS_PER_GROUP
D_EXPERT = 512
IN_A, IN_B, IN_C, IN_D = 2 * W_MIX, 3 * W_MIX, 2 * W_MIX, 3 * W_MIX
D_IN = IN_A + IN_B + IN_C + IN_D

LANES = 128
TOKEN_TILE = 768
Q_TILE = 512
CONV_BLOCK = 512
CONV_FFT = 2 * CONV_BLOCK
EXPERT_TILE = 512
MOD_ROWS_PAD = 8
VMEM_LIMIT = 56 * 1024 * 1024


def _cparams(sem, vmem=VMEM_LIMIT):
    return pltpu.CompilerParams(dimension_semantics=sem, vmem_limit_bytes=vmem)


def _dot(a, b):
    return jnp.dot(a, b, preferred_element_type=F32)


def _split_bf16(a):
    hi = a.astype(BF16)
    lo = (a - hi.astype(F32)).astype(BF16)
    return hi, lo


def _dot3(a, b):
    ah, al = _split_bf16(a)
    bh, bl = _split_bf16(b)
    return _dot(ah, bh) + _dot(al, bh) + _dot(ah, bl)


def _dot2_lhs(a, b_bf16):
    ah, al = _split_bf16(a)
    return _dot(ah, b_bf16) + _dot(al, b_bf16)


def _dot2_rhs(a_bf16, b):
    bh, bl = _split_bf16(b)
    return _dot(a_bf16, bh) + _dot(a_bf16, bl)


def _sigmoid(x):
    return 1.0 / (1.0 + jnp.exp(-x))


def _silu(x):
    return x * _sigmoid(x)


def _gelu_tanh(x):
    return 0.5 * x * (1.0 + jnp.tanh(math.sqrt(2.0 / math.pi) * (x + 0.044715 * (x * x * x))))


def _is_ctx_rows(tile_idx, tile_rows, n_lat):
    rows = tile_idx * tile_rows + lax.broadcasted_iota(jnp.int32, (tile_rows, 1), 0)
    return rows >= n_lat


def _mod_vec(mb_ref, mc_ref, k, is_ctx):
    vb = mb_ref[0, :, k * D_MODEL:(k + 1) * D_MODEL]
    vc = mc_ref[0, :, k * D_MODEL:(k + 1) * D_MODEL]
    return jnp.where(is_ctx, vc, vb)


def _rms_modulate(x, g, shift, scale):
    y = x * lax.rsqrt(jnp.mean(x * x, -1, keepdims=True) + EPS)
    return (y * g) * (1.0 + scale) + shift


def _ada_kernel(c_ref, w_ref, b_ref, o_ref):
    o_ref[0] = _dot3(_silu(c_ref[...]), w_ref[0]) + b_ref[0]


def _ada_table(c_all, w_ada, b_ada):
    depth, d, n = w_ada.shape
    rows = c_all.shape[0]
    tn = 1536
    return pl.pallas_call(
        _ada_kernel,
        out_shape=jax.ShapeDtypeStruct((depth, rows, n), F32),
        grid=(depth, n // tn),
        in_specs=[pl.BlockSpec((rows, d), lambda l, j: (0, 0)),
                  pl.BlockSpec((1, d, tn), lambda l, j: (l, 0, j)),
                  pl.BlockSpec((1, 1, tn), lambda l, j: (l, 0, j))],
        out_specs=pl.BlockSpec((1, rows, tn), lambda l, j: (l, 0, j)),
        compiler_params=_cparams(("arbitrary", "arbitrary")),
        name="ada_table",
    )(c_all, w_ada, b_ada.reshape(depth, 1, n))


def _in_proj_kernel(x_ref, mb_ref, mc_ref, g_ref, w_ref, pa_ref, pb_ref, pc_ref, pd_ref, *, n_lat):
    is_ctx = _is_ctx_rows(pl.program_id(1), TOKEN_TILE, n_lat)
    h = _rms_modulate(x_ref[0], g_ref[...], _mod_vec(mb_ref, mc_ref, 0, is_ctx),
                      _mod_vec(mb_ref, mc_ref, 1, is_ctx)).astype(BF16)
    col = 0
    for ref, width in ((pa_ref, IN_A), (pb_ref, IN_B), (pc_ref, IN_C), (pd_ref, IN_D)):
        ref[0] = _dot(h, w_ref[:, col:col + width]).astype(BF16)
        col += width


def _mod_specs(n_batch):
    width = N_MOD * D_MODEL
    return [pl.BlockSpec((1, 1, width), lambda b, j: (b, 0, 0)),
            pl.BlockSpec((1, 1, width), lambda b, j: (n_batch, 0, 0))]


def _in_proj(x, mod, g, w, n_lat):
    n_batch, s, d = x.shape
    widths = (IN_A, IN_B, IN_C, IN_D)
    return pl.pallas_call(
        functools.partial(_in_proj_kernel, n_lat=n_lat),
        out_shape=[jax.ShapeDtypeStruct((n_batch, s, wd), BF16) for wd in widths],
        grid=(n_batch, s // TOKEN_TILE),
        in_specs=[pl.BlockSpec((1, TOKEN_TILE, d), lambda b, j: (b, j, 0))] + _mod_specs(n_batch) + [
            pl.BlockSpec((1, d), lambda b, j: (0, 0)),
            pl.BlockSpec((d, D_IN), lambda b, j: (0, 0))],
        out_specs=[pl.BlockSpec((1, TOKEN_TILE, wd), lambda b, j: (b, j, 0)) for wd in widths],
        compiler_params=_cparams(("parallel", "arbitrary")),
        name="in_proj",
    )(x, mod, mod, g.reshape(1, d), w)


SCAN_SEGS = 8


LANE_CHUNKS = W_MIX // LANES


def _store_chunks(ref, value):
    rows = value.shape[0]
    for c in range(LANE_CHUNKS):
        ref[c * rows:(c + 1) * rows, :] = value[:, c * LANES:(c + 1) * LANES]


def _load_chunks(ref):
    rows = ref.shape[0] // LANE_CHUNKS
    return jnp.concatenate([ref[c * rows:(c + 1) * rows, :] for c in range(LANE_CHUNKS)], 1)


def _segmented_scan(a_ref, b_ref, p_ref, h_ref, row0, n, h0, reverse):
    seg = n // SCAN_SEGS
    chunk_rows = a_ref.shape[0] // LANE_CHUNKS

    def body(k, carry):
        kk = (seg - 1 - k) if reverse else k
        out = []
        for c in range(LANE_CHUNKS):
            h, p = carry[c]
            idx = pl.ds(c * chunk_rows + row0 + kk, SCAN_SEGS, stride=seg)
            a = a_ref[idx, :]
            h = a * h + b_ref[idx, :]
            p = a * p
            h_ref[idx, :] = h
            p_ref[idx, :] = p
            out.append((h, p))
        return tuple(out)

    init = tuple((jnp.zeros((SCAN_SEGS, LANES), F32), jnp.ones((SCAN_SEGS, LANES), F32))
                 for _ in range(LANE_CHUNKS))
    ends = lax.fori_loop(0, seg, body, init)
    final = []
    for c in range(LANE_CHUNKS):
        h_end, p_end = ends[c]
        carry = h0[c]
        carry_in = [None] * SCAN_SEGS
        for j in (range(SCAN_SEGS - 1, -1, -1) if reverse else range(SCAN_SEGS)):
            carry_in[j] = carry
            carry = h_end[j:j + 1] + p_end[j:j + 1] * carry
        for j in range(SCAN_SEGS):
            rows = pl.ds(c * chunk_rows + row0 + j * seg, seg)
            h_ref[rows, :] = h_ref[rows, :] + p_ref[rows, :] * carry_in[j]
        final.append(carry)
    return final


def _rglru_kernel(p_ref, cw_ref, cb_ref, wg_ref, bg_ref, lam_ref, o_ref,
                  xpad, a_s, b_s, p_s, hf_s, hb_s, *, n_lat, n_ctx):
    pad = 8
    lat0, ctx0 = pad, 2 * pad + n_lat
    zeros = jnp.zeros((pad, W_MIX), F32)
    xpad[0:pad, :] = zeros
    xpad[lat0:lat0 + n_lat, :] = p_ref[0, 0:n_lat, 0:W_MIX].astype(F32)
    xpad[lat0 + n_lat:ctx0, :] = zeros
    xpad[ctx0:ctx0 + n_ctx, :] = p_ref[0, n_lat:n_lat + n_ctx, 0:W_MIX].astype(F32)
    xpad[ctx0 + n_ctx:ctx0 + n_ctx + pad, :] = zeros

    for d, reverse in enumerate((False, True)):
        def conv(off, n):
            acc = cb_ref[d]
            for k in range(LRU_CONV):
                sh = k if reverse else k - (LRU_CONV - 1)
                acc = acc + cw_ref[d, k:k + 1, :] * xpad[off + sh:off + sh + n, :]
            return acc

        u = jnp.concatenate([conv(lat0, n_lat), conv(ctx0, n_ctx)], 0)
        gates = _dot(u.astype(BF16), wg_ref[d]) + bg_ref[d]
        r = _sigmoid(gates[:, 0:W_MIX])
        i = _sigmoid(gates[:, W_MIX:2 * W_MIX])
        z = -lam_ref[d]
        softplus = jnp.maximum(z, 0.0) + jnp.log(1.0 + jnp.exp(-jnp.abs(z)))
        a = jnp.exp(-RG_C * r * softplus)
        _store_chunks(a_s, a)
        _store_chunks(b_s, jnp.sqrt(1.0 - a * a) * (i * u))
        h_out = hb_s if reverse else hf_s
        zero_state = [jnp.zeros((1, LANES), F32)] * LANE_CHUNKS
        h_ctx = _segmented_scan(a_s, b_s, p_s, h_out, n_lat, n_ctx, zero_state, reverse)
        _segmented_scan(a_s, b_s, p_s, h_out, 0, n_lat, h_ctx, reverse)

    xg = p_ref[0, :, W_MIX:2 * W_MIX].astype(F32)
    o_ref[0] = ((_load_chunks(hf_s) + _load_chunks(hb_s)) * _gelu_tanh(xg)).astype(BF16)


def _rglru(p_a, conv_w, conv_b, w_gates, b_gates, lam, n_lat):
    n_batch, s, _ = p_a.shape
    n_ctx = s - n_lat
    full = lambda shape: pl.BlockSpec(shape, lambda b: (0,) * len(shape))
    return pl.pallas_call(
        functools.partial(_rglru_kernel, n_lat=n_lat, n_ctx=n_ctx),
        out_shape=jax.ShapeDtypeStruct((n_batch, s, W_MIX), BF16),
        grid=(n_batch,),
        in_specs=[pl.BlockSpec((1, s, IN_A), lambda b: (b, 0, 0)),
                  full((2, LRU_CONV, W_MIX)), full((2, 1, W_MIX)),
                  full((2, W_MIX, 2 * W_MIX)), full((2, 1, 2 * W_MIX)), full((2, 1, W_MIX))],
        out_specs=pl.BlockSpec((1, s, W_MIX), lambda b: (b, 0, 0)),
        scratch_shapes=[pltpu.VMEM((s + 24, W_MIX), F32)] + [pltpu.VMEM((LANE_CHUNKS * s, LANES), F32)] * 5,
        compiler_params=_cparams(("parallel",)),
        name="rglru",
    )(p_a, conv_w, conv_b, w_gates, b_gates, lam)


def _diff_attn_kernel(p_ref, cos_ref, sin_ref, lqk_ref, subg_ref, gmat_ref, o_ref, q_s, k_s,
                      *, n_lat, n_ctx, lam_init):
    lane = lax.broadcasted_iota(jnp.int32, (1, W_MIX), 1)
    first_half = (lane % DIFF_DK) < (DIFF_DK // 2)

    def rope(x):
        partner = jnp.where(first_half, pltpu.roll(x, W_MIX - DIFF_DK // 2, 1),
                            pltpu.roll(x, DIFF_DK // 2, 1))
        return x * cos_ref[...] + partner * sin_ref[...]

    scale = DIFF_DK ** -0.5 * math.log2(math.e)
    q = p_ref[0, :, 0:W_MIX].astype(F32)
    k = p_ref[0, :, W_MIX:2 * W_MIX].astype(F32)
    q_s[0:n_lat, :] = (rope(q[0:n_lat]) * scale).astype(BF16)
    q_s[n_lat:n_lat + n_ctx, :] = (q[n_lat:] * scale).astype(BF16)
    k_s[...] = jnp.concatenate([rope(k[0:n_lat]), k[n_lat:]], 0).T.astype(BF16)

    lqk = lqk_ref[...]
    lam = (jnp.exp(jnp.sum(lqk[0:1] * lqk[1:2], -1, keepdims=True))
           - jnp.exp(jnp.sum(lqk[2:3] * lqk[3:4], -1, keepdims=True)) + lam_init)

    def attend(q0, nq, kv0, nk):
        qt = q_s[pl.ds(q0, nq), :]
        keys_t = k_s[:, kv0:kv0 + nk]
        vals = p_ref[0, kv0:kv0 + nk, 2 * W_MIX:3 * W_MIX]
        acc = jnp.zeros((nq, W_MIX), F32)
        for h in range(DIFF_HEADS):
            e, r = [], []
            for c in range(2):
                qm = jnp.where(lane // DIFF_DK == 2 * h + c, qt, jnp.zeros_like(qt))
                s = _dot(qm, keys_t)
                e.append(jnp.exp2(s - jnp.max(s, -1, keepdims=True)))
                r.append(1.0 / jnp.sum(e[c], -1, keepdims=True))
            w = (e[0] * r[0] - e[1] * (lam * r[1])).astype(BF16)
            acc = jnp.where(lane // DIFF_DV == h, _dot(w, vals), acc)
        ms = _dot2_lhs(acc * acc, gmat_ref[...])
        y = acc * lax.rsqrt(ms + EPS) * subg_ref[...] * (1.0 - lam_init)
        o_ref[0, pl.ds(q0, nq), :] = y.astype(BF16)

    def lat_tile(i, carry):
        attend(pl.multiple_of(i * Q_TILE, Q_TILE), Q_TILE, 0, n_lat + n_ctx)
        return carry

    lax.fori_loop(0, n_lat // Q_TILE, lat_tile, 0)
    attend(n_lat, n_ctx, n_lat, n_ctx)


def _diff_attn(p_b, rope_cos, rope_sin, lqk, sub_g, gmat, n_lat, lam_init):
    n_batch, s, _ = p_b.shape
    n_ctx = s - n_lat
    full = lambda shape: pl.BlockSpec(shape, lambda b: (0,) * len(shape))
    return pl.pallas_call(
        functools.partial(_diff_attn_kernel, n_lat=n_lat, n_ctx=n_ctx, lam_init=lam_init),
        out_shape=jax.ShapeDtypeStruct((n_batch, s, W_MIX), BF16),
        grid=(n_batch,),
        in_specs=[pl.BlockSpec((1, s, IN_B), lambda b: (b, 0, 0)),
                  full((n_lat, W_MIX)), full((n_lat, W_MIX)), full((4, DIFF_DK)),
                  full((1, W_MIX)), full((W_MIX, W_MIX))],
        out_specs=pl.BlockSpec((1, s, W_MIX), lambda b: (b, 0, 0)),
        scratch_shapes=[pltpu.VMEM((s, W_MIX), BF16), pltpu.VMEM((W_MIX, s), BF16)],
        compiler_params=_cparams(("parallel",)),
        name="diff_attn",
    )(p_b, rope_cos, rope_sin, lqk, sub_g, gmat)


def _conformer_kernel(p_ref, cw_ref, cb_ref, lng_ref, lnb_ref, wpw_ref, bpw_ref, o_ref, upad,
                      *, n_lat, n_ctx):
    pad = 16
    half = CONF_K // 2
    lat0, ctx0 = pad, 2 * pad + n_lat
    val = p_ref[0, :, 0:W_MIX].astype(F32)
    gate = p_ref[0, :, W_MIX:2 * W_MIX].astype(F32)
    u = val * _sigmoid(gate)
    zeros = jnp.zeros((pad, W_MIX), F32)
    upad[0:pad, :] = zeros
    upad[lat0:lat0 + n_lat, :] = u[0:n_lat]
    upad[lat0 + n_lat:ctx0, :] = zeros
    upad[ctx0:ctx0 + n_ctx, :] = u[n_lat:]
    upad[ctx0 + n_ctx:ctx0 + n_ctx + pad, :] = zeros

    def conv(off, n):
        acc = cb_ref[...]
        for k in range(CONF_K):
            acc = acc + cw_ref[k:k + 1, :] * upad[off + k - half:off + k - half + n, :]
        return acc

    y = jnp.concatenate([conv(lat0, n_lat), conv(ctx0, n_ctx)], 0)
    mu = jnp.mean(y, -1, keepdims=True)
    yc = y - mu
    var = jnp.mean(yc * yc, -1, keepdims=True)
    y = yc * lax.rsqrt(var + 1e-5) * lng_ref[...] + lnb_ref[...]
    o_ref[0] = (_dot(_silu(y).astype(BF16), wpw_ref[...]) + bpw_ref[...]).astype(BF16)


def _conformer(p_c, conv_w, conv_b, ln_g, ln_b, w_pw, b_pw, n_lat):
    n_batch, s, _ = p_c.shape
    full = lambda shape: pl.BlockSpec(shape, lambda b: (0,) * len(shape))
    return pl.pallas_call(
        functools.partial(_conformer_kernel, n_lat=n_lat, n_ctx=s - n_lat),
        out_shape=jax.ShapeDtypeStruct((n_batch, s, W_MIX), BF16),
        grid=(n_batch,),
        in_specs=[pl.BlockSpec((1, s, IN_C), lambda b: (b, 0, 0)),
                  full((CONF_K, W_MIX)), full((1, W_MIX)), full((1, W_MIX)), full((1, W_MIX)),
                  full((W_MIX, W_MIX)), full((1, W_MIX))],
        out_specs=pl.BlockSpec((1, s, W_MIX), lambda b: (b, 0, 0)),
        scratch_shapes=[pltpu.VMEM((s + 48, W_MIX), F32)],
        compiler_params=_cparams(("parallel",)),
        name="conformer",
    )(p_c, conv_w, conv_b, ln_g, ln_b, w_pw, b_pw)


def _dft_matrices():
    n, hb = CONV_FFT, CONV_BLOCK
    k = np.arange(hb, dtype=np.float64)[:, None]
    t = np.arange(n, dtype=np.float64)[None, :]
    ang = 2.0 * np.pi * k * t / n
    full = np.concatenate([np.cos(ang), -np.sin(ang)], 0)
    full[hb, :] = np.cos(np.pi * t[0])
    fwd = full[:, :hb]
    wk = np.where(k == 0, 1.0, 2.0) / n
    tt = np.arange(hb, dtype=np.float64)[None, :]
    ang_i = 2.0 * np.pi * k * tt / n
    inv = np.concatenate([wk * np.cos(ang_i), -wk * np.sin(ang_i)], 0)
    inv[hb, :] = np.cos(np.pi * tt[0]) / n
    sign = np.cos(np.pi * k)
    filt = np.concatenate([sign * np.cos(ang), -sign * np.sin(ang)], 0)
    filt[hb, :] = np.cos(np.pi * t[0])
    filt[:, 0] = 0.0
    return fwd.astype(np.float32), inv.T.copy().astype(np.float32), filt.astype(np.float32)


def _hyena_positions(n_rows, center, seq_len):
    lag = np.arange(n_rows, dtype=np.float64) - center
    t = np.abs(lag)
    valid = t < seq_len
    t_unit = t / max(seq_len - 1, 1)
    bands = np.linspace(1e-4, HYENA_BANDS - 1, HYENA_BANDS)
    ang = (2.0 * np.pi / seq_len) * t[:, None] * bands[None, :]
    feats = np.zeros((n_rows, LANES), np.float64)
    feats[:, 0] = t_unit
    feats[:, 1:1 + HYENA_BANDS] = np.cos(ang)
    feats[:, 1 + HYENA_BANDS:HYENA_EMB] = -np.sin(ang)
    aux = np.zeros((n_rows, LANES), np.float64)
    aux[:, 0] = t_unit
    aux[:, 1] = valid
    aux[:, 2] = lag >= 0
    return feats.astype(np.float32), aux.astype(np.float32)


def _hyena_filter_kernel(fl_ref, al_ref, fc_ref, ac_ref, w1_ref, b1_ref, fr_ref, w2_ref, b2_ref,
                         w3_ref, dec_ref, filt_ref, a_lat, b_lat, n_lat, a_ctx, b_ctx, n_ctx):
    filt = filt_ref[...]
    row0 = lax.broadcasted_iota(jnp.int32, (CONV_BLOCK, 1), 0) == 0

    def taps(feats_ref, aux_ref):
        f = jnp.sin(fr_ref[0] * (_dot3(feats_ref[...], w1_ref[0]) + b1_ref[0]))
        f = jnp.sin(fr_ref[0] * (_dot3(f, w2_ref[0]) + b2_ref[0]))
        t_unit = aux_ref[:, 0:1]
        h = _dot3(f, w3_ref[0, 0]) * jnp.exp(-t_unit * jnp.abs(dec_ref[0, 0]))
        g = jnp.where(aux_ref[:, 2:3] > 0.5, h[:, 0:W_MIX], h[:, W_MIX:2 * W_MIX]) * aux_ref[:, 1:2]
        energy = jnp.sum(g * g, 0, keepdims=True)
        return g * lax.rsqrt(energy + EPS)

    def spectrum(window, a_out, b_out, n_out):
        hp = _dot2_rhs(filt, window)
        a_out[...] = hp[0:CONV_BLOCK]
        b_out[...] = jnp.where(row0, 0.0, hp[CONV_BLOCK:])
        n_out[...] = hp[CONV_BLOCK:CONV_BLOCK + 1]

    g = taps(fl_ref, al_ref)
    for dd in range(a_lat.shape[2]):
        spectrum(g[dd * CONV_BLOCK:dd * CONV_BLOCK + CONV_FFT],
                 a_lat.at[0, 0, dd], b_lat.at[0, 0, dd], n_lat.at[0, 0, dd])
    spectrum(taps(fc_ref, ac_ref), a_ctx.at[0, 0], b_ctx.at[0, 0], n_ctx.at[0, 0])


def _hyena_filters(feats_l, aux_l, feats_c, aux_c, w1, b1, freq, w2, b2, w3, decay, filt_m):
    depth = w1.shape[0]
    rows_l = feats_l.shape[0]
    n_off = rows_l // CONV_BLOCK - 1
    full = lambda shape: pl.BlockSpec(shape, lambda l, o: (0,) * len(shape))
    per_layer = lambda shape: pl.BlockSpec((1,) + shape, lambda l, o: (l,) + (0,) * len(shape))
    spec5 = lambda r: pl.BlockSpec((1, 1, n_off, r, W_MIX), lambda l, o: (l, o, 0, 0, 0))
    spec4 = lambda r: pl.BlockSpec((1, 1, r, W_MIX), lambda l, o: (l, o, 0, 0))
    sds = jax.ShapeDtypeStruct
    return pl.pallas_call(
        _hyena_filter_kernel,
        out_shape=[sds((depth, HYENA_ORDER, n_off, CONV_BLOCK, W_MIX), F32),
                   sds((depth, HYENA_ORDER, n_off, CONV_BLOCK, W_MIX), F32),
                   sds((depth, HYENA_ORDER, n_off, 1, W_MIX), F32),
                   sds((depth, HYENA_ORDER, CONV_BLOCK, W_MIX), F32),
                   sds((depth, HYENA_ORDER, CONV_BLOCK, W_MIX), F32),
                   sds((depth, HYENA_ORDER, 1, W_MIX), F32)],
        grid=(depth, HYENA_ORDER),
        in_specs=[full(feats_l.shape), full(aux_l.shape), full(feats_c.shape), full(aux_c.shape),
                  per_layer((LANES, HYENA_FFN)), per_layer((1, HYENA_FFN)), per_layer((1, HYENA_FFN)),
                  per_layer((HYENA_FFN, HYENA_FFN)), per_layer((1, HYENA_FFN)),
                  pl.BlockSpec((1, 1, HYENA_FFN, 2 * W_MIX), lambda l, o: (l, o, 0, 0)),
                  pl.BlockSpec((1, 1, 1, 2 * W_MIX), lambda l, o: (l, o, 0, 0)),
                  full(filt_m.shape)],
        out_specs=[spec5(CONV_BLOCK), spec5(CONV_BLOCK), spec5(1),
                   spec4(CONV_BLOCK), spec4(CONV_BLOCK), spec4(1)],
        compiler_params=_cparams(("arbitrary", "arbitrary")),
        name="hyena_filters",
    )(feats_l, aux_l, feats_c, aux_c, w1, b1, freq, w2, b2, w3, decay, filt_m)


def _hyena_kernel(p_ref, cw_ref, cb_ref, bias_ref, al_ref, bl_ref, nl_ref, ac_ref, bc_ref, nc_ref,
                  fwd_ref, inv_ref, o_ref, upad, *, n_lat, n_ctx):
    pad = 8
    lat0, ctx0 = pad, 2 * pad + n_lat
    n_blk = n_lat // CONV_BLOCK
    zeros = jnp.zeros((pad, IN_D), F32)
    upad[0:pad, :] = zeros
    upad[lat0:lat0 + n_lat, :] = p_ref[0, 0:n_lat, :].astype(F32)
    upad[lat0 + n_lat:ctx0, :] = zeros
    upad[ctx0:ctx0 + n_ctx, :] = p_ref[0, n_lat:n_lat + n_ctx, :].astype(F32)
    upad[ctx0 + n_ctx:ctx0 + n_ctx + pad, :] = zeros

    def conv(off, n):
        acc = cb_ref[...]
        for k in range(HYENA_SHORT):
            sh = k - HYENA_SHORT // 2
            acc = acc + cw_ref[k:k + 1, :] * upad[off + sh:off + sh + n, :]
        return acc

    u = jnp.concatenate([conv(lat0, n_lat), conv(ctx0, n_ctx)], 0)
    row0 = lax.broadcasted_iota(jnp.int32, (CONV_BLOCK, 1), 0) == 0
    fwd = fwd_ref[...]
    inv = inv_ref[...]

    def block_spectrum(zb):
        zf = _dot(fwd, zb.astype(BF16))
        return zf[0:CONV_BLOCK], zf[CONV_BLOCK:]

    def block_output(pre, pim, pnyq):
        pim = jnp.where(row0, pnyq, pim)
        return _dot(inv, jnp.concatenate([pre, pim], 0).astype(BF16))

    z = u[:, 0:W_MIX]
    for o in range(HYENA_ORDER):
        gate = u[:, (o + 1) * W_MIX:(o + 2) * W_MIX]
        spec = [block_spectrum(z[jb * CONV_BLOCK:(jb + 1) * CONV_BLOCK]) for jb in range(n_blk)]
        outs = []
        for ib in range(n_blk):
            pre = pim = pnyq = None
            for jb in range(n_blk):
                dd = ib - jb + n_blk - 1
                a, b, nq = al_ref[o, dd], bl_ref[o, dd], nl_ref[o, dd]
                zre, zim = spec[jb]
                t_re = zre * a - zim * b
                t_im = zre * b + zim * a
                t_ny = zim[0:1] * nq
                pre = t_re if pre is None else pre + t_re
                pim = t_im if pim is None else pim + t_im
                pnyq = t_ny if pnyq is None else pnyq + t_ny
            outs.append(block_output(pre, pim, pnyq))
        zc = jnp.concatenate([z[n_lat:], jnp.zeros((CONV_BLOCK - n_ctx, W_MIX), F32)], 0)
        zre, zim = block_spectrum(zc)
        a, b, nq = ac_ref[o], bc_ref[o], nc_ref[o]
        outs.append(block_output(zre * a - zim * b, zre * b + zim * a, zim[0:1] * nq)[0:n_ctx])
        y = jnp.concatenate(outs, 0)
        z = gate * (y + bias_ref[o] * z)
    o_ref[0] = z.astype(BF16)


def _hyena(p_d, conv_w, conv_b, bias, filters, fwd_m, inv_m, n_lat):
    n_batch, s, _ = p_d.shape
    n_off = 2 * (n_lat // CONV_BLOCK) - 1
    full = lambda shape: pl.BlockSpec(shape, lambda b: (0,) * len(shape))
    a_lat, b_lat, n_lat_f, a_ctx, b_ctx, n_ctx_f = filters
    return pl.pallas_call(
        functools.partial(_hyena_kernel, n_lat=n_lat, n_ctx=s - n_lat),
        out_shape=jax.ShapeDtypeStruct((n_batch, s, W_MIX), BF16),
        grid=(n_batch,),
        in_specs=[pl.BlockSpec((1, s, IN_D), lambda b: (b, 0, 0)),
                  full((HYENA_SHORT, IN_D)), full((1, IN_D)), full((HYENA_ORDER, 1, W_MIX)),
                  full((HYENA_ORDER, n_off, CONV_BLOCK, W_MIX)), full((HYENA_ORDER, n_off, CONV_BLOCK, W_MIX)),
                  full((HYENA_ORDER, n_off, 1, W_MIX)),
                  full((HYENA_ORDER, CONV_BLOCK, W_MIX)), full((HYENA_ORDER, CONV_BLOCK, W_MIX)),
                  full((HYENA_ORDER, 1, W_MIX)),
                  full((CONV_FFT, CONV_BLOCK)), full((CONV_BLOCK, CONV_FFT))],
        out_specs=pl.BlockSpec((1, s, W_MIX), lambda b: (b, 0, 0)),
        scratch_shapes=[pltpu.VMEM((s + 24, IN_D), F32)],
        compiler_params=_cparams(("parallel",)),
        name="hyena",
    )(p_d, conv_w, conv_b, bias, a_lat, b_lat, n_lat_f, a_ctx, b_ctx, n_ctx_f, fwd_m, inv_m)


ROUTE_P0, ROUTE_P1, ROUTE_G0, ROUTE_G1 = range(4)
RUN_ALIGN = 8
SORT_ROWS = 2 * TOKEN_TILE + N_EXPERTS * RUN_ALIGN
HALF_D = D_MODEL // 2


def _pack_pairs(v):
    bits = lax.bitcast_convert_type(v.astype(BF16).astype(F32), jnp.int32)
    return lax.shift_right_logical(bits[:, 0:HALF_D], 16) | bits[:, HALF_D:]


def _unpack_pairs(w):
    lo = lax.bitcast_convert_type(lax.shift_left(w, 16), F32).astype(BF16)
    hi = lax.bitcast_convert_type(w & jnp.int32(-65536), F32).astype(BF16)
    return lo, hi


def _out_route_kernel(x_ref, ya_ref, yb_ref, yc_ref, yd_ref, mb_ref, mc_ref, g_ref, wo_ref, wr_ref,
                      br_ref, tri_ref, upper_ref, xo_ref, hs_ref, route_ref, cnt_ref, *, n_lat, route_ctx):
    is_ctx = _is_ctx_rows(pl.program_id(1), TOKEN_TILE, n_lat)
    mix = None
    for g, ref in enumerate((ya_ref, yb_ref, yc_ref, yd_ref)):
        part = _dot(ref[0], wo_ref[g * W_MIX:(g + 1) * W_MIX, :])
        mix = part if mix is None else mix + part
    x = x_ref[0] + _mod_vec(mb_ref, mc_ref, 2, is_ctx) * mix
    xo_ref[0] = x
    h = _rms_modulate(x, g_ref[...], _mod_vec(mb_ref, mc_ref, 3, is_ctx), _mod_vec(mb_ref, mc_ref, 4, is_ctx))

    logits = _dot3(h, wr_ref[...]) + br_ref[...]
    lane = lax.broadcasted_iota(jnp.int32, (1, LANES), 1)
    neg = jnp.float32(-jnp.inf)

    def top1(vals):
        m = jnp.max(vals, -1, keepdims=True)
        idx = jnp.min(jnp.where(vals == m, lane, LANES), -1, keepdims=True)
        return m, idx

    grp = jnp.where(lane < N_GROUPS, logits, neg)
    m_g, g_idx = top1(grp)
    p_grp = 1.0 / jnp.sum(jnp.exp(grp - m_g), -1, keepdims=True)
    lo = N_GROUPS + g_idx * EXPERTS_PER_GROUP
    exp_l = jnp.where((lane >= lo) & (lane < lo + EXPERTS_PER_GROUP), logits, neg)
    m1, i1 = top1(exp_l)
    m2, i2 = top1(jnp.where(lane == i1, neg, exp_l))
    e2 = jnp.exp(m2 - m1)
    gate0 = p_grp / (1.0 + e2)
    gate1 = p_grp * e2 / (1.0 + e2)

    live = jnp.ones_like(is_ctx) if route_ctx else jnp.logical_not(is_ctx)
    oh0 = jnp.where((lane == i1) & live, 1.0, 0.0)
    oh1 = jnp.where((lane == i2) & live, 1.0, 0.0)
    both = jnp.concatenate([oh0, oh1], 1).astype(BF16)
    before = _dot(tri_ref[...], both)
    cnt0 = jnp.sum(oh0, 0, keepdims=True)
    cnt = cnt0 + jnp.sum(oh1, 0, keepdims=True)
    run = jnp.floor((cnt + (RUN_ALIGN - 1)) * (1.0 / RUN_ALIGN)) * RUN_ALIGN
    cnt_ref[0] = run.astype(jnp.int32)
    lower = _dot2_lhs(jnp.broadcast_to(run, (8, LANES)), upper_ref[...])[0:1]
    pos0 = jnp.sum((before[:, 0:LANES] + lower) * oh0, -1, keepdims=True)
    pos1 = jnp.sum((before[:, LANES:] + cnt0 + lower) * oh1, -1, keepdims=True)
    pos0 = jnp.where(live, pos0, -1.0)
    pos1 = jnp.where(live, pos1, -1.0)

    slab = jnp.zeros((TOKEN_TILE, LANES), F32)
    for col, v in ((ROUTE_P0, pos0), (ROUTE_P1, pos1), (ROUTE_G0, gate0), (ROUTE_G1, gate1)):
        slab = jnp.where(lane == col, v, slab)
    route_ref[0] = slab

    slab_t = slab.T
    q = lax.broadcasted_iota(jnp.int32, (SORT_ROWS, 1), 0).astype(F32)
    sel = (q == slab_t[ROUTE_P0:ROUTE_P0 + 1]) | (q == slab_t[ROUTE_P1:ROUTE_P1 + 1])
    sel = jnp.where(sel, 1.0, 0.0).astype(BF16)
    hs_ref[0] = _pack_pairs(_dot(sel, h.astype(BF16)))


def _out_route(x, ys, mod, g, w_out, w_route, b_route, tri, upper, n_lat, route_ctx):
    n_batch, s, d = x.shape
    tiles = s // TOKEN_TILE
    tok = lambda width: pl.BlockSpec((1, TOKEN_TILE, width), lambda b, j: (b, j, 0))
    full = lambda shape: pl.BlockSpec(shape, lambda b, j: (0,) * len(shape))
    per_tile = lambda rows, width: pl.BlockSpec((1, rows, width), lambda b, j: (b * tiles + j, 0, 0))
    sds = jax.ShapeDtypeStruct
    return pl.pallas_call(
        functools.partial(_out_route_kernel, n_lat=n_lat, route_ctx=route_ctx),
        out_shape=[sds((n_batch, s, d), F32), sds((n_batch * tiles, SORT_ROWS, HALF_D), jnp.int32),
                   sds((n_batch, s, LANES), F32), sds((n_batch * tiles, 1, LANES), jnp.int32)],
        grid=(n_batch, tiles),
        in_specs=[tok(d)] + [tok(W_MIX)] * 4 + _mod_specs(n_batch) + [
            full((1, d)), full((4 * W_MIX, d)), full((d, LANES)), full((1, LANES)),
            full((TOKEN_TILE, TOKEN_TILE)), full((LANES, LANES))],
        out_specs=[tok(d), per_tile(SORT_ROWS, HALF_D), tok(LANES), per_tile(1, LANES)],
        input_output_aliases={0: 0},
        compiler_params=_cparams(("parallel", "arbitrary")),
        name="out_route",
    )(x, *ys, mod, mod, g.reshape(1, d), w_out, w_route, b_route, tri, upper)


def _split3_bf16(a):
    p1 = a.astype(BF16)
    r1 = a - p1.astype(F32)
    p2 = r1.astype(BF16)
    return p1, p2, (r1 - p2.astype(F32)).astype(BF16)


def _moe_meta_kernel(cnt_ref, tril_ref, upper_ref, src_ref, dst_ref, blk_ref, used_ref, *, n_blocks):
    n_tiles = cnt_ref.shape[0]
    cnt = cnt_ref[...].astype(F32)
    upper = upper_ref[...]
    tiles_before = _dot2_rhs(tril_ref[...], cnt)
    total = jnp.sum(cnt, 0, keepdims=True)
    padded = jnp.floor((total + (EXPERT_TILE - 1)) * (1.0 / EXPERT_TILE)) * EXPERT_TILE
    parts = _split3_bf16(jnp.broadcast_to(padded, (8, LANES)))
    start = (_dot(parts[0], upper) + _dot(parts[1], upper) + _dot(parts[2], upper))[0:1]
    tile_row = lax.broadcasted_iota(jnp.int32, (n_tiles, 1), 0).astype(F32)
    src_ref[...] = (tile_row * SORT_ROWS + _dot2_lhs(cnt, upper)).astype(jnp.int32)
    dst_ref[...] = (start + tiles_before).astype(jnp.int32)
    lane = lax.broadcasted_iota(jnp.int32, (1, LANES), 1)
    is_expert = (lane >= N_GROUPS) & (lane < N_GROUPS + N_EXPERTS)
    end = start + padded
    blk0 = lax.broadcasted_iota(jnp.int32, (n_blocks, 1), 0).astype(F32) * EXPERT_TILE
    done = jnp.sum(jnp.where(is_expert & (end <= blk0), 1.0, 0.0), -1, keepdims=True)
    blk_ref[...] = jnp.minimum(done, N_EXPERTS - 1.0).astype(jnp.int32)
    used_ref[...] = (jnp.sum(padded, -1, keepdims=True) * (1.0 / EXPERT_TILE)).astype(jnp.int32)


def _moe_meta(counts, tril_tiles, upper, n_blocks):
    n_tiles = counts.shape[0]
    sds = jax.ShapeDtypeStruct
    src, dst, blk, used = pl.pallas_call(
        functools.partial(_moe_meta_kernel, n_blocks=n_blocks),
        out_shape=[sds((n_tiles, LANES), jnp.int32), sds((n_tiles, LANES), jnp.int32),
                   sds((n_blocks, 1), jnp.int32), sds((1, 1), jnp.int32)],
        name="moe_meta",
    )(counts.reshape(n_tiles, LANES), tril_tiles, upper)
    flat = lambda tab: tab[:, N_GROUPS:N_GROUPS + N_EXPERTS].reshape(n_tiles * N_EXPERTS)
    return (flat(counts.reshape(n_tiles, LANES)), flat(src), flat(dst)), blk.reshape(n_blocks), used.reshape(1)


RUN_BITS = SORT_ROWS.bit_length()


def _run_copies(n, src, src_row, dst, dst_row, sem, fn):
    for b in range(RUN_BITS - 1, RUN_ALIGN.bit_length() - 2, -1):
        size = 1 << b
        done = lax.shift_left(lax.shift_right_logical(n, b + 1), b + 1)

        @pl.when((n & size) != 0)
        def _():
            fn(pltpu.make_async_copy(src.at[pl.ds(pl.multiple_of(src_row + done, RUN_ALIGN), size), :],
                                     dst.at[pl.ds(pl.multiple_of(dst_row + done, RUN_ALIGN), size), :], sem))


def _tile_runs(cnt_ref, src_ref, dst_ref, tile, sorted_ref, slots_ref, sem, fn, to_slots, sorted_base=0):
    for e in range(N_EXPERTS):
        k = tile * N_EXPERTS + e
        in_sorted = src_ref[k] - sorted_base
        if to_slots:
            _run_copies(cnt_ref[k], sorted_ref, in_sorted, slots_ref, dst_ref[k], sem, fn)
        else:
            _run_copies(cnt_ref[k], slots_ref, dst_ref[k], sorted_ref, in_sorted, sem, fn)


def _to_slots_kernel(cnt_ref, src_ref, dst_ref, hs_ref, zero_ref, xs_ref, sem):
    del zero_ref
    t = pl.program_id(0)
    runs = functools.partial(_tile_runs, cnt_ref, src_ref, dst_ref, sorted_ref=hs_ref, slots_ref=xs_ref,
                             sem=sem, to_slots=True)
    runs(tile=t, fn=lambda cp: cp.start())

    @pl.when(t > 0)
    def _():
        runs(tile=t - 1, fn=lambda cp: cp.wait())

    @pl.when(t == pl.num_programs(0) - 1)
    def _():
        runs(tile=t, fn=lambda cp: cp.wait())


def _to_slots(tables, hs_sorted, n_slots):
    n_tiles = hs_sorted.shape[0]
    return pl.pallas_call(
        _to_slots_kernel,
        out_shape=jax.ShapeDtypeStruct((n_slots, HALF_D), jnp.int32),
        grid_spec=pltpu.PrefetchScalarGridSpec(
            num_scalar_prefetch=3,
            grid=(n_tiles,),
            in_specs=[pl.BlockSpec(memory_space=pl.ANY), pl.BlockSpec(memory_space=pl.ANY)],
            out_specs=pl.BlockSpec(memory_space=pl.ANY),
            scratch_shapes=[pltpu.SemaphoreType.DMA(())]),
        input_output_aliases={4: 0},
        compiler_params=_cparams(("arbitrary",)),
        name="moe_to_slots",
    )(*tables, hs_sorted.reshape(n_tiles * SORT_ROWS, HALF_D), jnp.zeros((n_slots, HALF_D), jnp.int32))


def _expert_kernel(be_ref, nb_ref, x_ref, wgu_ref, wd_ref, y_ref):
    i = pl.program_id(0)

    @pl.when(i < nb_ref[0])
    def _():
        lo, hi = _unpack_pairs(x_ref[...])
        gu = _dot(lo, wgu_ref[0, 0:HALF_D, :]) + _dot(hi, wgu_ref[0, HALF_D:, :])
        hid = _silu(gu[:, 0:D_EXPERT]) * gu[:, D_EXPERT:]
        y_ref[...] = _pack_pairs(_dot(hid.astype(BF16), wd_ref[0]))

    @pl.when(i >= nb_ref[0])
    def _():
        y_ref[...] = jnp.zeros_like(y_ref)


def _experts(blk_expert, n_used, xs, w_gu, w_down):
    n_slots = xs.shape[0]
    return pl.pallas_call(
        _expert_kernel,
        out_shape=jax.ShapeDtypeStruct((n_slots, HALF_D), jnp.int32),
        grid_spec=pltpu.PrefetchScalarGridSpec(
            num_scalar_prefetch=2,
            grid=(n_slots // EXPERT_TILE,),
            in_specs=[pl.BlockSpec((EXPERT_TILE, HALF_D), lambda i, be, nb: (i, 0)),
                      pl.BlockSpec((1, D_MODEL, 2 * D_EXPERT), lambda i, be, nb: (be[i], 0, 0)),
                      pl.BlockSpec((1, D_EXPERT, D_MODEL), lambda i, be, nb: (be[i], 0, 0))],
            out_specs=pl.BlockSpec((EXPERT_TILE, HALF_D), lambda i, be, nb: (i, 0))),
        compiler_params=_cparams(("arbitrary",)),
        name="moe_experts",
    )(blk_expert, n_used, xs, w_gu, w_down)


def _combine_kernel(cnt_ref, src_ref, dst_ref, x_ref, route_ref, mb_ref, mc_ref, ys_ref, xo_ref, buf, sem,
                    *, n_lat):
    tiles = pl.num_programs(1)
    t = pl.program_id(0) * tiles + pl.program_id(1)
    buf[...] = jnp.zeros_like(buf)
    runs = functools.partial(_tile_runs, cnt_ref, src_ref, dst_ref, tile=t, sorted_ref=buf, slots_ref=ys_ref,
                             sem=sem, to_slots=False, sorted_base=t * SORT_ROWS)
    runs(fn=lambda cp: cp.start())
    route = route_ref[0]
    col = lax.broadcasted_iota(jnp.int32, (1, SORT_ROWS), 1).astype(F32)
    weights = (jnp.where(col == route[:, ROUTE_P0:ROUTE_P0 + 1], route[:, ROUTE_G0:ROUTE_G0 + 1], 0.0)
               + jnp.where(col == route[:, ROUTE_P1:ROUTE_P1 + 1], route[:, ROUTE_G1:ROUTE_G1 + 1], 0.0)
               ).astype(BF16)
    runs(fn=lambda cp: cp.wait())
    lo, hi = _unpack_pairs(buf[...])
    moe = jnp.concatenate([_dot(weights, lo), _dot(weights, hi)], 1)
    is_ctx = _is_ctx_rows(pl.program_id(1), TOKEN_TILE, n_lat)
    xo_ref[0] = x_ref[0] + _mod_vec(mb_ref, mc_ref, 5, is_ctx) * moe


def _combine(x, route, mod, tables, ys, n_lat):
    n_batch, s, d = x.shape
    tiles = s // TOKEN_TILE
    tok = lambda width: pl.BlockSpec((1, TOKEN_TILE, width), lambda b, j, *_: (b, j, 0))
    mod_specs = [pl.BlockSpec((1, 1, N_MOD * d), lambda b, j, *_: (b, 0, 0)),
                 pl.BlockSpec((1, 1, N_MOD * d), lambda b, j, *_: (n_batch, 0, 0))]
    return pl.pallas_call(
        functools.partial(_combine_kernel, n_lat=n_lat),
        out_shape=jax.ShapeDtypeStruct((n_batch, s, d), F32),
        grid_spec=pltpu.PrefetchScalarGridSpec(
            num_scalar_prefetch=3,
            grid=(n_batch, tiles),
            in_specs=[tok(d), tok(LANES)] + mod_specs + [pl.BlockSpec(memory_space=pl.ANY)],
            out_specs=tok(d),
            scratch_shapes=[pltpu.VMEM((SORT_ROWS, HALF_D), jnp.int32), pltpu.SemaphoreType.DMA(())]),
        input_output_aliases={3: 0},
        compiler_params=_cparams(("arbitrary", "arbitrary")),
        name="moe_combine",
    )(*tables, x, route, mod, mod, ys)


def _final_kernel(x_ref, g_ref, o_ref):
    x = x_ref[0]
    o_ref[0] = x * lax.rsqrt(jnp.mean(x * x, -1, keepdims=True) + EPS) * g_ref[...]


def _final_norm(x, g, n_lat):
    n_batch, _, d = x.shape
    tm = 512
    return pl.pallas_call(
        _final_kernel,
        out_shape=jax.ShapeDtypeStruct((n_batch, n_lat, d), F32),
        grid=(n_batch, n_lat // tm),
        in_specs=[pl.BlockSpec((1, tm, d), lambda b, j: (b, j, 0)), pl.BlockSpec((1, d), lambda b, j: (0, 0))],
        out_specs=pl.BlockSpec((1, tm, d), lambda b, j: (b, j, 0)),
        compiler_params=_cparams(("parallel", "arbitrary")),
        name="final_norm",
    )(x, g.reshape(1, d))


def _block_diag(w):
    heads, di, dj = w.shape
    eye = jnp.eye(heads, dtype=w.dtype)
    return (eye[:, None, :, None] * w[:, :, None, :]).reshape(heads * di, heads * dj)


def _rope_tables(n_lat):
    rows = n_lat // GRID_W
    row = np.repeat(np.arange(rows, dtype=np.float64), GRID_W)
    col = np.tile(np.arange(GRID_W, dtype=np.float64), rows)
    inv = ROPE_BASE ** (-np.arange(ROPE_FREQS, dtype=np.float64) / ROPE_FREQS)
    ang = np.concatenate([row[:, None] * inv, col[:, None] * inv], -1)
    cos = np.tile(np.concatenate([np.cos(ang), np.cos(ang)], -1), (1, W_MIX // DIFF_DK))
    sin = np.tile(np.concatenate([-np.sin(ang), np.sin(ang)], -1), (1, W_MIX // DIFF_DK))
    return jnp.asarray(cos, F32), jnp.asarray(sin, F32)


def kernel(x, c, ctx, c_ctx, w_ada, b_ada, g_mix, g_ffn, w_in, w_out, a_conv_w, a_conv_b, a_w_r, a_b_r,
           a_w_i, a_b_i, a_lam, b_lq1, b_lk1, b_lq2, b_lk2, b_sub_g, c_conv_w, c_conv_b, c_ln_g, c_ln_b,
           c_w_pw, c_b_pw, d_conv_w, d_conv_b, d_w_f1, d_b_f1, d_freq, d_w_f2, d_b_f2, d_w_f3, d_decay,
           d_bias, moe_w_rg, moe_b_rg, moe_w_re, moe_b_re, moe_w_gate, moe_w_up, moe_w_down, g_final):
    n_batch, n_lat, d = x.shape
    n_ctx = ctx.shape[1]
    depth = w_ada.shape[0]
    s = n_lat + n_ctx
    assert d == D_MODEL and s % TOKEN_TILE == 0 and n_lat % CONV_BLOCK == 0 and n_ctx <= CONV_BLOCK
    assert n_lat % Q_TILE == 0 and n_ctx % LANES == 0 and n_lat % GRID_W == 0

    xs = jnp.concatenate([x, ctx], 1)
    mod_rows = -(-(n_batch + 1) // MOD_ROWS_PAD) * MOD_ROWS_PAD
    c_all = jnp.concatenate([c, c_ctx[None], jnp.zeros((mod_rows - n_batch - 1, d), F32)], 0)
    mod_all = _ada_table(c_all, w_ada, b_ada)

    rope_cos, rope_sin = _rope_tables(n_lat)
    fwd_np, inv_np, filt_np = _dft_matrices()
    fwd_m, inv_m, filt_m = (jnp.asarray(m, F32).astype(BF16) for m in (fwd_np, inv_np, filt_np))
    feats_l, aux_l = (jnp.asarray(m) for m in _hyena_positions(2 * n_lat, n_lat, n_lat))
    feats_c, aux_c = (jnp.asarray(m) for m in _hyena_positions(CONV_FFT, CONV_BLOCK, n_ctx))
    gmat = jnp.asarray(np.kron(np.eye(DIFF_HEADS), np.full((DIFF_DV, DIFF_DV), 1.0 / DIFF_DV)), F32).astype(BF16)
    tri = jnp.asarray(np.tril(np.ones((TOKEN_TILE, TOKEN_TILE)), -1), F32).astype(BF16)
    n_tiles = n_batch * (s // TOKEN_TILE)
    tril_tiles = jnp.asarray(np.tril(np.ones((n_tiles, n_tiles)), -1), F32).astype(BF16)
    upper = jnp.asarray(np.triu(np.ones((LANES, LANES)), 1), F32).astype(BF16)

    w1 = jnp.pad(d_w_f1, ((0, 0), (0, LANES - HYENA_EMB), (0, 0)))
    w3 = d_w_f3.reshape(depth, HYENA_FFN, HYENA_ORDER, 2 * W_MIX).transpose(0, 2, 1, 3)
    dec = d_decay.reshape(depth, HYENA_ORDER, 1, 2 * W_MIX)
    row = lambda v: v.reshape(depth, 1, -1)
    hy = _hyena_filters(feats_l, aux_l, feats_c, aux_c, w1, row(d_b_f1), row(d_freq), d_w_f2, row(d_b_f2),
                        w3, dec, filt_m)

    n_tok = n_batch * s
    for l in range(depth):
        last = l == depth - 1
        lam_init = 0.8 - 0.6 * math.exp(-0.3 * l)
        mod = mod_all[l].reshape(mod_rows, 1, N_MOD * d)
        p_a, p_b, p_c, p_d = _in_proj(xs, mod, g_mix[l], w_in[l].astype(BF16), n_lat)

        w_gates = jnp.stack([jnp.concatenate([_block_diag(a_w_r[l, dr]), _block_diag(a_w_i[l, dr])], 1)
                             for dr in range(2)]).astype(BF16)
        b_gates = jnp.concatenate([a_b_r[l], a_b_i[l]], -1)[:, None, :]
        y_a = _rglru(p_a, a_conv_w[l], a_conv_b[l][:, None, :], w_gates, b_gates, a_lam[l][:, None, :], n_lat)
        lqk = jnp.stack([b_lq1[l], b_lk1[l], b_lq2[l], b_lk2[l]])
        y_b = _diff_attn(p_b, rope_cos, rope_sin, lqk, jnp.tile(b_sub_g[l], DIFF_HEADS)[None], gmat,
                         n_lat, lam_init)
        y_c = _conformer(p_c, c_conv_w[l], c_conv_b[l][None], c_ln_g[l][None], c_ln_b[l][None],
                         c_w_pw[l].astype(BF16), c_b_pw[l][None], n_lat)
        y_d = _hyena(p_d, d_conv_w[l], d_conv_b[l][None], d_bias[l][:, None, :], [f[l] for f in hy],
                     fwd_m, inv_m, n_lat)

        w_route = jnp.pad(jnp.concatenate([moe_w_rg[l], moe_w_re[l]], 1),
                          ((0, 0), (0, LANES - N_GROUPS - N_EXPERTS)))
        b_route = jnp.pad(jnp.concatenate([moe_b_rg[l], moe_b_re[l]]), (0, LANES - N_GROUPS - N_EXPERTS))[None]
        xs, hs_sorted, route, counts = _out_route(xs, (y_a, y_b, y_c, y_d), mod, g_ffn[l], w_out[l].astype(BF16),
                                                  w_route, b_route, tri, upper, n_lat, not last)

        n_assign = 2 * (n_tok if not last else n_batch * n_lat)
        n_rows_max = n_assign + n_tiles * N_EXPERTS * (RUN_ALIGN - 1)
        n_slots = -(-n_rows_max // EXPERT_TILE) * EXPERT_TILE + N_EXPERTS * EXPERT_TILE
        tables, blk_expert, n_used = _moe_meta(counts, tril_tiles, upper, n_slots // EXPERT_TILE)
        slots = _to_slots(tables, hs_sorted, n_slots)
        w_gu = jnp.concatenate([moe_w_gate[l], moe_w_up[l]], -1).astype(BF16)
        ys = _experts(blk_expert, n_used, slots, w_gu, moe_w_down[l].astype(BF16))
        xs = _combine(xs, route, mod, tables, ys, n_lat)

    return _final_norm(xs, g_final, n_lat)
```

```python
import functools
import math

import numpy as np
import jax
import jax.numpy as jnp
from jax import lax
from jax.experimental import pallas as pl
from jax.experimental.pallas import tpu as pltpu

F32 = jnp.float32
BF16 = jnp.bfloat16

D_MODEL = 1024
N_MOD = 6
EPS = 1e-6
W_MIX = 256
LRU_HEADS = 4
LRU_CONV = 4
RG_C = 8.0
DIFF_HEADS = 4
DIFF_DV = W_MIX // DIFF_HEADS
DIFF_DK = DIFF_DV // 2
ROPE_BASE = 10000.0
ROPE_FREQS = DIFF_DK // 4
GRID_W = 64
CONF_K = 31
HYENA_ORDER = 2
HYENA_SHORT = 3
HYENA_EMB = 33
HYENA_BANDS = (HYENA_EMB - 1) // 2
HYENA_FFN = 64
N_GROUPS = 4
EXPERTS_PER_GROUP = 8
N_EXPERTS = N_GROUPS * EXPERTS_PER_GROUP
D_EXPERT = 512
IN_A, IN_B, IN_C, IN_D = 2 * W_MIX, 3 * W_MIX, 2 * W_MIX, 3 * W_MIX
D_IN = IN_A + IN_B + IN_C + IN_D

LANES = 128
TOKEN_TILE = 768
Q_TILE = 512
CONV_BLOCK = 512
CONV_FFT = 2 * CONV_BLOCK
EXPERT_TILE = 512
MOD_ROWS_PAD = 8
VMEM_LIMIT = 56 * 1024 * 1024


def _cparams(sem, vmem=VMEM_LIMIT):
    return pltpu.CompilerParams(dimension_semantics=sem, vmem_limit_bytes=vmem)


def _dot(a, b):
    return jnp.dot(a, b, preferred_element_type=F32)


def _split_bf16(a):
    hi = a.astype(BF16)
    lo = (a - hi.astype(F32)).astype(BF16)
    return hi, lo


def _dot3(a, b):
    ah, al = _split_bf16(a)
    bh, bl = _split_bf16(b)
    return _dot(ah, bh) + _dot(al, bh) + _dot(ah, bl)


def _dot2_lhs(a, b_bf16):
    ah, al = _split_bf16(a)
    return _dot(ah, b_bf16) + _dot(al, b_bf16)


def _dot2_rhs(a_bf16, b):
    bh, bl = _split_bf16(b)
    return _dot(a_bf16, bh) + _dot(a_bf16, bl)


def _sigmoid(x):
    return 1.0 / (1.0 + jnp.exp(-x))


def _silu(x):
    return x * _sigmoid(x)


def _gelu_tanh(x):
    return 0.5 * x * (1.0 + jnp.tanh(math.sqrt(2.0 / math.pi) * (x + 0.044715 * (x * x * x))))


def _is_ctx_rows(tile_idx, tile_rows, n_lat):
    rows = tile_idx * tile_rows + lax.broadcasted_iota(jnp.int32, (tile_rows, 1), 0)
    return rows >= n_lat


def _mod_vec(mb_ref, mc_ref, k, is_ctx):
    vb = mb_ref[0, :, k * D_MODEL:(k + 1) * D_MODEL]
    vc = mc_ref[0, :, k * D_MODEL:(k + 1) * D_MODEL]
    return jnp.where(is_ctx, vc, vb)


def _rms_modulate(x, g, shift, scale):
    y = x * lax.rsqrt(jnp.mean(x * x, -1, keepdims=True) + EPS)
    return (y * g) * (1.0 + scale) + shift


def _ada_kernel(c_ref, w_ref, b_ref, o_ref):
    o_ref[0] = _dot3(_silu(c_ref[...]), w_ref[0]) + b_ref[0]


def _ada_table(c_all, w_ada, b_ada):
    depth, d, n = w_ada.shape
    rows = c_all.shape[0]
    tn = 1536
    return pl.pallas_call(
        _ada_kernel,
        out_shape=jax.ShapeDtypeStruct((depth, rows, n), F32),
        grid=(depth, n // tn),
        in_specs=[pl.BlockSpec((rows, d), lambda l, j: (0, 0)),
                  pl.BlockSpec((1, d, tn), lambda l, j: (l, 0, j)),
                  pl.BlockSpec((1, 1, tn), lambda l, j: (l, 0, j))],
        out_specs=pl.BlockSpec((1, rows, tn), lambda l, j: (l, 0, j)),
        compiler_params=_cparams(("arbitrary", "arbitrary")),
        name="ada_table",
    )(c_all, w_ada, b_ada.reshape(depth, 1, n))


def _in_proj_kernel(x_ref, mb_ref, mc_ref, g_ref, w_ref, pa_ref, pb_ref, pc_ref, pd_ref, *, n_lat):
    is_ctx = _is_ctx_rows(pl.program_id(1), TOKEN_TILE, n_lat)
    h = _rms_modulate(x_ref[0], g_ref[...], _mod_vec(mb_ref, mc_ref, 0, is_ctx),
                      _mod_vec(mb_ref, mc_ref, 1, is_ctx)).astype(BF16)
    col = 0
    for ref, width in ((pa_ref, IN_A), (pb_ref, IN_B), (pc_ref, IN_C), (pd_ref, IN_D)):
        ref[0] = _dot(h, w_ref[:, col:col + width]).astype(BF16)
        col += width


def _mod_specs(n_batch):
    width = N_MOD * D_MODEL
    return [pl.BlockSpec((1, 1, width), lambda b, j: (b, 0, 0)),
            pl.BlockSpec((1, 1, width), lambda b, j: (n_batch, 0, 0))]


def _in_proj(x, mod, g, w, n_lat):
    n_batch, s, d = x.shape
    widths = (IN_A, IN_B, IN_C, IN_D)
    return pl.pallas_call(
        functools.partial(_in_proj_kernel, n_lat=n_lat),
        out_shape=[jax.ShapeDtypeStruct((n_batch, s, wd), BF16) for wd in widths],
        grid=(n_batch, s // TOKEN_TILE),
        in_specs=[pl.BlockSpec((1, TOKEN_TILE, d), lambda b, j: (b, j, 0))] + _mod_specs(n_batch) + [
            pl.BlockSpec((1, d), lambda b, j: (0, 0)),
            pl.BlockSpec((d, D_IN), lambda b, j: (0, 0))],
        out_specs=[pl.BlockSpec((1, TOKEN_TILE, wd), lambda b, j: (b, j, 0)) for wd in widths],
        compiler_params=_cparams(("parallel", "arbitrary")),
        name="in_proj",
    )(x, mod, mod, g.reshape(1, d), w)


SUBLANES = 8


def _group_scan(a, b, reverse):
    n, w = a.shape
    a = a.reshape(n // SUBLANES, SUBLANES, w)
    b = b.reshape(n // SUBLANES, SUBLANES, w)
    row = lax.broadcasted_iota(jnp.int32, (1, SUBLANES, 1), 1)
    d = 1
    while d < SUBLANES:
        keep = (row < SUBLANES - d) if reverse else (row >= d)
        shift = SUBLANES - d if reverse else d
        b = a * jnp.where(keep, pltpu.roll(b, shift, 1), 0.0) + b
        a = a * jnp.where(keep, pltpu.roll(a, shift, 1), 1.0)
        d *= 2
    return a.reshape(n, w), b.reshape(n, w)


def _chain_groups(a_ref, b_ref, h_ref, row0, n, h0, reverse):
    groups = n // SUBLANES

    def body(g, carry):
        gg = (groups - 1 - g) if reverse else g
        rows = pl.ds(pl.multiple_of(row0 + gg * SUBLANES, SUBLANES), SUBLANES)
        h = b_ref[rows, :] + a_ref[rows, :] * carry
        h_ref[rows, :] = h
        return h[0:1] if reverse else h[SUBLANES - 1:SUBLANES]

    return lax.fori_loop(0, groups, body, h0)


def _rglru_kernel(p_ref, cw_ref, cb_ref, wg_ref, bg_ref, lam_ref, o_ref,
                  xpad, a_s, b_s, hf_s, hb_s, *, n_lat, n_ctx):
    pad = 8
    lat0, ctx0 = pad, 2 * pad + n_lat
    zeros = jnp.zeros((pad, W_MIX), F32)
    xpad[0:pad, :] = zeros
    xpad[lat0:lat0 + n_lat, :] = p_ref[0, 0:n_lat, 0:W_MIX].astype(F32)
    xpad[lat0 + n_lat:ctx0, :] = zeros
    xpad[ctx0:ctx0 + n_ctx, :] = p_ref[0, n_lat:n_lat + n_ctx, 0:W_MIX].astype(F32)
    xpad[ctx0 + n_ctx:ctx0 + n_ctx + pad, :] = zeros

    for d, reverse in enumerate((False, True)):
        def conv(off, n):
            acc = cb_ref[d]
            for k in range(LRU_CONV):
                sh = k if reverse else k - (LRU_CONV - 1)
                acc = acc + cw_ref[d, k:k + 1, :] * xpad[off + sh:off + sh + n, :]
            return acc

        u = jnp.concatenate([conv(lat0, n_lat), conv(ctx0, n_ctx)], 0)
        gates = _dot(u.astype(BF16), wg_ref[d]) + bg_ref[d]
        r = _sigmoid(gates[:, 0:W_MIX])
        i = _sigmoid(gates[:, W_MIX:2 * W_MIX])
        z = -lam_ref[d]
        softplus = jnp.maximum(z, 0.0) + jnp.log(1.0 + jnp.exp(-jnp.abs(z)))
        a = jnp.exp(-RG_C * r * softplus)
        b = jnp.sqrt(1.0 - a * a) * (i * u)
        a_s[...], b_s[...] = _group_scan(a, b, reverse)
        h_out = hb_s if reverse else hf_s
        h_ctx = _chain_groups(a_s, b_s, h_out, n_lat, n_ctx, jnp.zeros((1, W_MIX), F32), reverse)
        _chain_groups(a_s, b_s, h_out, 0, n_lat, h_ctx, reverse)

    xg = p_ref[0, :, W_MIX:2 * W_MIX].astype(F32)
    o_ref[0] = ((hf_s[...] + hb_s[...]) * _gelu_tanh(xg)).astype(BF16)


def _rglru(p_a, conv_w, conv_b, w_gates, b_gates, lam, n_lat):
    n_batch, s, _ = p_a.shape
    n_ctx = s - n_lat
    full = lambda shape: pl.BlockSpec(shape, lambda b: (0,) * len(shape))
    return pl.pallas_call(
        functools.partial(_rglru_kernel, n_lat=n_lat, n_ctx=n_ctx),
        out_shape=jax.ShapeDtypeStruct((n_batch, s, W_MIX), BF16),
        grid=(n_batch,),
        in_specs=[pl.BlockSpec((1, s, IN_A), lambda b: (b, 0, 0)),
                  full((2, LRU_CONV, W_MIX)), full((2, 1, W_MIX)),
                  full((2, W_MIX, 2 * W_MIX)), full((2, 1, 2 * W_MIX)), full((2, 1, W_MIX))],
        out_specs=pl.BlockSpec((1, s, W_MIX), lambda b: (b, 0, 0)),
        scratch_shapes=[pltpu.VMEM((s + 24, W_MIX), F32)] + [pltpu.VMEM((s, W_MIX), F32)] * 4,
        compiler_params=_cparams(("parallel",)),
        name="rglru",
    )(p_a, conv_w, conv_b, w_gates, b_gates, lam)


def _diff_attn_kernel(p_ref, cos_ref, sin_ref, lqk_ref, subg_ref, gmat_ref, o_ref, q_s, k_s,
                      *, n_lat, n_ctx, lam_init):
    lane = lax.broadcasted_iota(jnp.int32, (1, W_MIX), 1)
    first_half = (lane % DIFF_DK) < (DIFF_DK // 2)

    def rope(x):
        partner = jnp.where(first_half, pltpu.roll(x, W_MIX - DIFF_DK // 2, 1),
                            pltpu.roll(x, DIFF_DK // 2, 1))
        return x * cos_ref[...] + partner * sin_ref[...]

    scale = DIFF_DK ** -0.5 * math.log2(math.e)
    q = p_ref[0, :, 0:W_MIX].astype(F32)
    k = p_ref[0, :, W_MIX:2 * W_MIX].astype(F32)
    q_s[0:n_lat, :] = (rope(q[0:n_lat]) * scale).astype(BF16)
    q_s[n_lat:n_lat + n_ctx, :] = (q[n_lat:] * scale).astype(BF16)
    k_s[...] = jnp.concatenate([rope(k[0:n_lat]), k[n_lat:]], 0).T.astype(BF16)

    lqk = lqk_ref[...]
    lam = (jnp.exp(jnp.sum(lqk[0:1] * lqk[1:2], -1, keepdims=True))
           - jnp.exp(jnp.sum(lqk[2:3] * lqk[3:4], -1, keepdims=True)) + lam_init)

    def attend(q0, nq, kv0, nk):
        qt = q_s[pl.ds(q0, nq), :]
        keys_t = k_s[:, kv0:kv0 + nk]
        vals = p_ref[0, kv0:kv0 + nk, 2 * W_MIX:3 * W_MIX]
        acc = jnp.zeros((nq, W_MIX), F32)
        for h in range(DIFF_HEADS):
            e, r = [], []
            for c in range(2):
                qm = jnp.where(lane // DIFF_DK == 2 * h + c, qt, jnp.zeros_like(qt))
                s = _dot(qm, keys_t)
                e.append(jnp.exp2(s - jnp.max(s, -1, keepdims=True)))
                r.append(1.0 / jnp.sum(e[c], -1, keepdims=True))
            w = (e[0] * r[0] - e[1] * (lam * r[1])).astype(BF16)
            acc = jnp.where(lane // DIFF_DV == h, _dot(w, vals), acc)
        ms = _dot2_lhs(acc * acc, gmat_ref[...])
        y = acc * lax.rsqrt(ms + EPS) * subg_ref[...] * (1.0 - lam_init)
        o_ref[0, pl.ds(q0, nq), :] = y.astype(BF16)

    def lat_tile(i, carry):
        attend(pl.multiple_of(i * Q_TILE, Q_TILE), Q_TILE, 0, n_lat + n_ctx)
        return carry

    lax.fori_loop(0, n_lat // Q_TILE, lat_tile, 0)
    attend(n_lat, n_ctx, n_lat, n_ctx)


def _diff_attn(p_b, rope_cos, rope_sin, lqk, sub_g, gmat, n_lat, lam_init):
    n_batch, s, _ = p_b.shape
    n_ctx = s - n_lat
    full = lambda shape: pl.BlockSpec(shape, lambda b: (0,) * len(shape))
    return pl.pallas_call(
        functools.partial(_diff_attn_kernel, n_lat=n_lat, n_ctx=n_ctx, lam_init=lam_init),
        out_shape=jax.ShapeDtypeStruct((n_batch, s, W_MIX), BF16),
        grid=(n_batch,),
        in_specs=[pl.BlockSpec((1, s, IN_B), lambda b: (b, 0, 0)),
                  full((n_lat, W_MIX)), full((n_lat, W_MIX)), full((4, DIFF_DK)),
                  full((1, W_MIX)), full((W_MIX, W_MIX))],
        out_specs=pl.BlockSpec((1, s, W_MIX), lambda b: (b, 0, 0)),
        scratch_shapes=[pltpu.VMEM((s, W_MIX), BF16), pltpu.VMEM((W_MIX, s), BF16)],
        compiler_params=_cparams(("parallel",)),
        name="diff_attn",
    )(p_b, rope_cos, rope_sin, lqk, sub_g, gmat)


def _conformer_kernel(p_ref, cw_ref, cb_ref, lng_ref, lnb_ref, wpw_ref, bpw_ref, o_ref, upad,
                      *, n_lat, n_ctx):
    pad = 16
    half = CONF_K // 2
    lat0, ctx0 = pad, 2 * pad + n_lat
    val = p_ref[0, :, 0:W_MIX].astype(F32)
    gate = p_ref[0, :, W_MIX:2 * W_MIX].astype(F32)
    u = val * _sigmoid(gate)
    zeros = jnp.zeros((pad, W_MIX), F32)
    upad[0:pad, :] = zeros
    upad[lat0:lat0 + n_lat, :] = u[0:n_lat]
    upad[lat0 + n_lat:ctx0, :] = zeros
    upad[ctx0:ctx0 + n_ctx, :] = u[n_lat:]
    upad[ctx0 + n_ctx:ctx0 + n_ctx + pad, :] = zeros

    def conv(off, n):
        acc = cb_ref[...]
        for k in range(CONF_K):
            acc = acc + cw_ref[k:k + 1, :] * upad[off + k - half:off + k - half + n, :]
        return acc

    y = jnp.concatenate([conv(lat0, n_lat), conv(ctx0, n_ctx)], 0)
    mu = jnp.mean(y, -1, keepdims=True)
    yc = y - mu
    var = jnp.mean(yc * yc, -1, keepdims=True)
    y = yc * lax.rsqrt(var + 1e-5) * lng_ref[...] + lnb_ref[...]
    o_ref[0] = (_dot(_silu(y).astype(BF16), wpw_ref[...]) + bpw_ref[...]).astype(BF16)


def _conformer(p_c, conv_w, conv_b, ln_g, ln_b, w_pw, b_pw, n_lat):
    n_batch, s, _ = p_c.shape
    full = lambda shape: pl.BlockSpec(shape, lambda b: (0,) * len(shape))
    return pl.pallas_call(
        functools.partial(_conformer_kernel, n_lat=n_lat, n_ctx=s - n_lat),
        out_shape=jax.ShapeDtypeStruct((n_batch, s, W_MIX), BF16),
        grid=(n_batch,),
        in_specs=[pl.BlockSpec((1, s, IN_C), lambda b: (b, 0, 0)),
                  full((CONF_K, W_MIX)), full((1, W_MIX)), full((1, W_MIX)), full((1, W_MIX)),
                  full((W_MIX, W_MIX)), full((1, W_MIX))],
        out_specs=pl.BlockSpec((1, s, W_MIX), lambda b: (b, 0, 0)),
        scratch_shapes=[pltpu.VMEM((s + 48, W_MIX), F32)],
        compiler_params=_cparams(("parallel",)),
        name="conformer",
    )(p_c, conv_w, conv_b, ln_g, ln_b, w_pw, b_pw)


def _dft_matrices():
    n, hb = CONV_FFT, CONV_BLOCK
    k = np.arange(hb, dtype=np.float64)[:, None]
    t = np.arange(n, dtype=np.float64)[None, :]
    ang = 2.0 * np.pi * k * t / n
    full = np.concatenate([np.cos(ang), -np.sin(ang)], 0)
    full[hb, :] = np.cos(np.pi * t[0])
    fwd = full[:, :hb]
    wk = np.where(k == 0, 1.0, 2.0) / n
    tt = np.arange(hb, dtype=np.float64)[None, :]
    ang_i = 2.0 * np.pi * k * tt / n
    inv = np.concatenate([wk * np.cos(ang_i), -wk * np.sin(ang_i)], 0)
    inv[hb, :] = np.cos(np.pi * tt[0]) / n
    sign = np.cos(np.pi * k)
    filt = np.concatenate([sign * np.cos(ang), -sign * np.sin(ang)], 0)
    filt[hb, :] = np.cos(np.pi * t[0])
    filt[:, 0] = 0.0
    return fwd.astype(np.float32), inv.T.copy().astype(np.float32), filt.astype(np.float32)


def _hyena_positions(n_rows, center, seq_len):
    lag = np.arange(n_rows, dtype=np.float64) - center
    t = np.abs(lag)
    valid = t < seq_len
    t_unit = t / max(seq_len - 1, 1)
    bands = np.linspace(1e-4, HYENA_BANDS - 1, HYENA_BANDS)
    ang = (2.0 * np.pi / seq_len) * t[:, None] * bands[None, :]
    feats = np.zeros((n_rows, LANES), np.float64)
    feats[:, 0] = t_unit
    feats[:, 1:1 + HYENA_BANDS] = np.cos(ang)
    feats[:, 1 + HYENA_BANDS:HYENA_EMB] = -np.sin(ang)
    aux = np.zeros((n_rows, LANES), np.float64)
    aux[:, 0] = t_unit
    aux[:, 1] = valid
    aux[:, 2] = lag >= 0
    return feats.astype(np.float32), aux.astype(np.float32)


def _hyena_filter_kernel(fl_ref, al_ref, fc_ref, ac_ref, w1_ref, b1_ref, fr_ref, w2_ref, b2_ref,
                         w3_ref, dec_ref, filt_ref, a_lat, b_lat, n_lat, a_ctx, b_ctx, n_ctx):
    filt = filt_ref[...]
    row0 = lax.broadcasted_iota(jnp.int32, (CONV_BLOCK, 1), 0) == 0

    def taps(feats_ref, aux_ref):
        f = jnp.sin(fr_ref[0] * (_dot3(feats_ref[...], w1_ref[0]) + b1_ref[0]))
        f = jnp.sin(fr_ref[0] * (_dot3(f, w2_ref[0]) + b2_ref[0]))
        t_unit = aux_ref[:, 0:1]
        h = _dot3(f, w3_ref[0, 0]) * jnp.exp(-t_unit * jnp.abs(dec_ref[0, 0]))
        g = jnp.where(aux_ref[:, 2:3] > 0.5, h[:, 0:W_MIX], h[:, W_MIX:2 * W_MIX]) * aux_ref[:, 1:2]
        energy = jnp.sum(g * g, 0, keepdims=True)
        return g * lax.rsqrt(energy + EPS)

    def spectrum(window, a_out, b_out, n_out):
        hp = _dot2_rhs(filt, window)
        a_out[...] = hp[0:CONV_BLOCK]
        b_out[...] = jnp.where(row0, 0.0, hp[CONV_BLOCK:])
        n_out[...] = hp[CONV_BLOCK:CONV_BLOCK + 1]

    g = taps(fl_ref, al_ref)
    for dd in range(a_lat.shape[2]):
        spectrum(g[dd * CONV_BLOCK:dd * CONV_BLOCK + CONV_FFT],
                 a_lat.at[0, 0, dd], b_lat.at[0, 0, dd], n_lat.at[0, 0, dd])
    spectrum(taps(fc_ref, ac_ref), a_ctx.at[0, 0], b_ctx.at[0, 0], n_ctx.at[0, 0])


def _hyena_filters(feats_l, aux_l, feats_c, aux_c, w1, b1, freq, w2, b2, w3, decay, filt_m):
    depth = w1.shape[0]
    rows_l = feats_l.shape[0]
    n_off = rows_l // CONV_BLOCK - 1
    full = lambda shape: pl.BlockSpec(shape, lambda l, o: (0,) * len(shape))
    per_layer = lambda shape: pl.BlockSpec((1,) + shape, lambda l, o: (l,) + (0,) * len(shape))
    spec5 = lambda r: pl.BlockSpec((1, 1, n_off, r, W_MIX), lambda l, o: (l, o, 0, 0, 0))
    spec4 = lambda r: pl.BlockSpec((1, 1, r, W_MIX), lambda l, o: (l, o, 0, 0))
    sds = jax.ShapeDtypeStruct
    return pl.pallas_call(
        _hyena_filter_kernel,
        out_shape=[sds((depth, HYENA_ORDER, n_off, CONV_BLOCK, W_MIX), F32),
                   sds((depth, HYENA_ORDER, n_off, CONV_BLOCK, W_MIX), F32),
                   sds((depth, HYENA_ORDER, n_off, 1, W_MIX), F32),
                   sds((depth, HYENA_ORDER, CONV_BLOCK, W_MIX), F32),
                   sds((depth, HYENA_ORDER, CONV_BLOCK, W_MIX), F32),
                   sds((depth, HYENA_ORDER, 1, W_MIX), F32)],
        grid=(depth, HYENA_ORDER),
        in_specs=[full(feats_l.shape), full(aux_l.shape), full(feats_c.shape), full(aux_c.shape),
                  per_layer((LANES, HYENA_FFN)), per_layer((1, HYENA_FFN)), per_layer((1, HYENA_FFN)),
                  per_layer((HYENA_FFN, HYENA_FFN)), per_layer((1, HYENA_FFN)),
                  pl.BlockSpec((1, 1, HYENA_FFN, 2 * W_MIX), lambda l, o: (l, o, 0, 0)),
                  pl.BlockSpec((1, 1, 1, 2 * W_MIX), lambda l, o: (l, o, 0, 0)),
                  full(filt_m.shape)],
        out_specs=[spec5(CONV_BLOCK), spec5(CONV_BLOCK), spec5(1),
                   spec4(CONV_BLOCK), spec4(CONV_BLOCK), spec4(1)],
        compiler_params=_cparams(("arbitrary", "arbitrary")),
        name="hyena_filters",
    )(feats_l, aux_l, feats_c, aux_c, w1, b1, freq, w2, b2, w3, decay, filt_m)


def _hyena_kernel(p_ref, cw_ref, cb_ref, bias_ref, al_ref, bl_ref, nl_ref, ac_ref, bc_ref, nc_ref,
                  fwd_ref, inv_ref, o_ref, upad, *, n_lat, n_ctx):
    pad = 8
    lat0, ctx0 = pad, 2 * pad + n_lat
    n_blk = n_lat // CONV_BLOCK
    zeros = jnp.zeros((pad, IN_D), F32)
    upad[0:pad, :] = zeros
    upad[lat0:lat0 + n_lat, :] = p_ref[0, 0:n_lat, :].astype(F32)
    upad[lat0 + n_lat:ctx0, :] = zeros
    upad[ctx0:ctx0 + n_ctx, :] = p_ref[0, n_lat:n_lat + n_ctx, :].astype(F32)
    upad[ctx0 + n_ctx:ctx0 + n_ctx + pad, :] = zeros

    def conv(off, n):
        acc = cb_ref[...]
        for k in range(HYENA_SHORT):
            sh = k - HYENA_SHORT // 2
            acc = acc + cw_ref[k:k + 1, :] * upad[off + sh:off + sh + n, :]
        return acc

    u = jnp.concatenate([conv(lat0, n_lat), conv(ctx0, n_ctx)], 0)
    row0 = lax.broadcasted_iota(jnp.int32, (CONV_BLOCK, 1), 0) == 0
    fwd = fwd_ref[...]
    inv = inv_ref[...]

    def block_spectrum(zb):
        zf = _dot(fwd, zb.astype(BF16))
        return zf[0:CONV_BLOCK], zf[CONV_BLOCK:]

    def block_output(pre, pim, pnyq):
        pim = jnp.where(row0, pnyq, pim)
        return _dot(inv, jnp.concatenate([pre, pim], 0).astype(BF16))

    z = u[:, 0:W_MIX]
    for o in range(HYENA_ORDER):
        gate = u[:, (o + 1) * W_MIX:(o + 2) * W_MIX]
        spec = [block_spectrum(z[jb * CONV_BLOCK:(jb + 1) * CONV_BLOCK]) for jb in range(n_blk)]
        outs = []
        for ib in range(n_blk):
            pre = pim = pnyq = None
            for jb in range(n_blk):
                dd = ib - jb + n_blk - 1
                a, b, nq = al_ref[o, dd], bl_ref[o, dd], nl_ref[o, dd]
                zre, zim = spec[jb]
                t_re = zre * a - zim * b
                t_im = zre * b + zim * a
                t_ny = zim[0:1] * nq
                pre = t_re if pre is None else pre + t_re
                pim = t_im if pim is None else pim + t_im
                pnyq = t_ny if pnyq is None else pnyq + t_ny
            outs.append(block_output(pre, pim, pnyq))
        zc = jnp.concatenate([z[n_lat:], jnp.zeros((CONV_BLOCK - n_ctx, W_MIX), F32)], 0)
        zre, zim = block_spectrum(zc)
        a, b, nq = ac_ref[o], bc_ref[o], nc_ref[o]
        outs.append(block_output(zre * a - zim * b, zre * b + zim * a, zim[0:1] * nq)[0:n_ctx])
        y = jnp.concatenate(outs, 0)
        z = gate * (y + bias_ref[o] * z)
    o_ref[0] = z.astype(BF16)


def _hyena(p_d, conv_w, conv_b, bias, filters, fwd_m, inv_m, n_lat):
    n_batch, s, _ = p_d.shape
    n_off = 2 * (n_lat // CONV_BLOCK) - 1
    full = lambda shape: pl.BlockSpec(shape, lambda b: (0,) * len(shape))
    a_lat, b_lat, n_lat_f, a_ctx, b_ctx, n_ctx_f = filters
    return pl.pallas_call(
        functools.partial(_hyena_kernel, n_lat=n_lat, n_ctx=s - n_lat),
        out_shape=jax.ShapeDtypeStruct((n_batch, s, W_MIX), BF16),
        grid=(n_batch,),
        in_specs=[pl.BlockSpec((1, s, IN_D), lambda b: (b, 0, 0)),
                  full((HYENA_SHORT, IN_D)), full((1, IN_D)), full((HYENA_ORDER, 1, W_MIX)),
                  full((HYENA_ORDER, n_off, CONV_BLOCK, W_MIX)), full((HYENA_ORDER, n_off, CONV_BLOCK, W_MIX)),
                  full((HYENA_ORDER, n_off, 1, W_MIX)),
                  full((HYENA_ORDER, CONV_BLOCK, W_MIX)), full((HYENA_ORDER, CONV_BLOCK, W_MIX)),
                  full((HYENA_ORDER, 1, W_MIX)),
                  full((CONV_FFT, CONV_BLOCK)), full((CONV_BLOCK, CONV_FFT))],
        out_specs=pl.BlockSpec((1, s, W_MIX), lambda b: (b, 0, 0)),
        scratch_shapes=[pltpu.VMEM((s + 24, IN_D), F32)],
        compiler_params=_cparams(("parallel",)),
        name="hyena",
    )(p_d, conv_w, conv_b, bias, a_lat, b_lat, n_lat_f, a_ctx, b_ctx, n_ctx_f, fwd_m, inv_m)


ROUTE_P0, ROUTE_P1, ROUTE_G0, ROUTE_G1 = range(4)
RUN_ALIGN = 8
SORT_ROWS = 2 * TOKEN_TILE + N_EXPERTS * RUN_ALIGN
HALF_D = D_MODEL // 2


def _pack_pairs(v):
    bits = lax.bitcast_convert_type(v.astype(BF16).astype(F32), jnp.int32)
    return lax.shift_right_logical(bits[:, 0:HALF_D], 16) | bits[:, HALF_D:]


def _unpack_pairs(w):
    lo = lax.bitcast_convert_type(lax.shift_left(w, 16), F32).astype(BF16)
    hi = lax.bitcast_convert_type(w & jnp.int32(-65536), F32).astype(BF16)
    return lo, hi


def _out_route_kernel(x_ref, ya_ref, yb_ref, yc_ref, yd_ref, mb_ref, mc_ref, g_ref, wo_ref, wr_ref,
                      br_ref, tri_ref, upper_ref, xo_ref, hs_ref, route_ref, cnt_ref, *, n_lat, route_ctx):
    is_ctx = _is_ctx_rows(pl.program_id(1), TOKEN_TILE, n_lat)
    mix = None
    for g, ref in enumerate((ya_ref, yb_ref, yc_ref, yd_ref)):
        part = _dot(ref[0], wo_ref[g * W_MIX:(g + 1) * W_MIX, :])
        mix = part if mix is None else mix + part
    x = x_ref[0] + _mod_vec(mb_ref, mc_ref, 2, is_ctx) * mix
    xo_ref[0] = x
    h = _rms_modulate(x, g_ref[...], _mod_vec(mb_ref, mc_ref, 3, is_ctx), _mod_vec(mb_ref, mc_ref, 4, is_ctx))

    logits = _dot3(h, wr_ref[...]) + br_ref[...]
    lane = lax.broadcasted_iota(jnp.int32, (1, LANES), 1)
    neg = jnp.float32(-jnp.inf)

    def top1(vals):
        m = jnp.max(vals, -1, keepdims=True)
        idx = jnp.min(jnp.where(vals == m, lane, LANES), -1, keepdims=True)
        return m, idx

    grp = jnp.where(lane < N_GROUPS, logits, neg)
    m_g, g_idx = top1(grp)
    p_grp = 1.0 / jnp.sum(jnp.exp(grp - m_g), -1, keepdims=True)
    lo = N_GROUPS + g_idx * EXPERTS_PER_GROUP
    exp_l = jnp.where((lane >= lo) & (lane < lo + EXPERTS_PER_GROUP), logits, neg)
    m1, i1 = top1(exp_l)
    m2, i2 = top1(jnp.where(lane == i1, neg, exp_l))
    e2 = jnp.exp(m2 - m1)
    gate0 = p_grp / (1.0 + e2)
    gate1 = p_grp * e2 / (1.0 + e2)

    live = jnp.ones_like(is_ctx) if route_ctx else jnp.logical_not(is_ctx)
    oh0 = jnp.where((lane == i1) & live, 1.0, 0.0)
    oh1 = jnp.where((lane == i2) & live, 1.0, 0.0)
    both = jnp.concatenate([oh0, oh1], 1).astype(BF16)
    before = _dot(tri_ref[...], both)
    cnt0 = jnp.sum(oh0, 0, keepdims=True)
    cnt = cnt0 + jnp.sum(oh1, 0, keepdims=True)
    run = jnp.floor((cnt + (RUN_ALIGN - 1)) * (1.0 / RUN_ALIGN)) * RUN_ALIGN
    cnt_ref[0] = run.astype(jnp.int32)
    lower = _dot2_lhs(jnp.broadcast_to(run, (8, LANES)), upper_ref[...])[0:1]
    pos0 = jnp.sum((before[:, 0:LANES] + lower) * oh0, -1, keepdims=True)
    pos1 = jnp.sum((before[:, LANES:] + cnt0 + lower) * oh1, -1, keepdims=True)
    pos0 = jnp.where(live, pos0, -1.0)
    pos1 = jnp.where(live, pos1, -1.0)

    slab = jnp.zeros((TOKEN_TILE, LANES), F32)
    for col, v in ((ROUTE_P0, pos0), (ROUTE_P1, pos1), (ROUTE_G0, gate0), (ROUTE_G1, gate1)):
        slab = jnp.where(lane == col, v, slab)
    route_ref[0] = slab

    slab_t = slab.T
    q = lax.broadcasted_iota(jnp.int32, (SORT_ROWS, 1), 0).astype(F32)
    sel = (q == slab_t[ROUTE_P0:ROUTE_P0 + 1]) | (q == slab_t[ROUTE_P1:ROUTE_P1 + 1])
    sel = jnp.where(sel, 1.0, 0.0).astype(BF16)
    hs_ref[0] = _pack_pairs(_dot(sel, h.astype(BF16)))


def _out_route(x, ys, mod, g, w_out, w_route, b_route, tri, upper, n_lat, route_ctx):
    n_batch, s, d = x.shape
    tiles = s // TOKEN_TILE
    tok = lambda width: pl.BlockSpec((1, TOKEN_TILE, width), lambda b, j: (b, j, 0))
    full = lambda shape: pl.BlockSpec(shape, lambda b, j: (0,) * len(shape))
    per_tile = lambda rows, width: pl.BlockSpec((1, rows, width), lambda b, j: (b * tiles + j, 0, 0))
    sds = jax.ShapeDtypeStruct
    return pl.pallas_call(
        functools.partial(_out_route_kernel, n_lat=n_lat, route_ctx=route_ctx),
        out_shape=[sds((n_batch, s, d), F32), sds((n_batch * tiles, SORT_ROWS, HALF_D), jnp.int32),
                   sds((n_batch, s, LANES), F32), sds((n_batch * tiles, 1, LANES), jnp.int32)],
        grid=(n_batch, tiles),
        in_specs=[tok(d)] + [tok(W_MIX)] * 4 + _mod_specs(n_batch) + [
            full((1, d)), full((4 * W_MIX, d)), full((d, LANES)), full((1, LANES)),
            full((TOKEN_TILE, TOKEN_TILE)), full((LANES, LANES))],
        out_specs=[tok(d), per_tile(SORT_ROWS, HALF_D), tok(LANES), per_tile(1, LANES)],
        input_output_aliases={0: 0},
        compiler_params=_cparams(("parallel", "arbitrary")),
        name="out_route",
    )(x, *ys, mod, mod, g.reshape(1, d), w_out, w_route, b_route, tri, upper)


def _split3_bf16(a):
    p1 = a.astype(BF16)
    r1 = a - p1.astype(F32)
    p2 = r1.astype(BF16)
    return p1, p2, (r1 - p2.astype(F32)).astype(BF16)


def _moe_meta_kernel(cnt_ref, tril_ref, upper_ref, src_ref, dst_ref, blk_ref, used_ref, *, n_blocks):
    n_tiles = cnt_ref.shape[0]
    cnt = cnt_ref[...].astype(F32)
    upper = upper_ref[...]
    tiles_before = _dot2_rhs(tril_ref[...], cnt)
    total = jnp.sum(cnt, 0, keepdims=True)
    padded = jnp.floor((total + (EXPERT_TILE - 1)) * (1.0 / EXPERT_TILE)) * EXPERT_TILE
    parts = _split3_bf16(jnp.broadcast_to(padded, (8, LANES)))
    start = (_dot(parts[0], upper) + _dot(parts[1], upper) + _dot(parts[2], upper))[0:1]
    tile_row = lax.broadcasted_iota(jnp.int32, (n_tiles, 1), 0).astype(F32)
    src_ref[...] = (tile_row * SORT_ROWS + _dot2_lhs(cnt, upper)).astype(jnp.int32)
    dst_ref[...] = (start + tiles_before).astype(jnp.int32)
    lane = lax.broadcasted_iota(jnp.int32, (1, LANES), 1)
    is_expert = (lane >= N_GROUPS) & (lane < N_GROUPS + N_EXPERTS)
    end = start + padded
    blk0 = lax.broadcasted_iota(jnp.int32, (n_blocks, 1), 0).astype(F32) * EXPERT_TILE
    done = jnp.sum(jnp.where(is_expert & (end <= blk0), 1.0, 0.0), -1, keepdims=True)
    blk_ref[...] = jnp.minimum(done, N_EXPERTS - 1.0).astype(jnp.int32)
    used_ref[...] = (jnp.sum(padded, -1, keepdims=True) * (1.0 / EXPERT_TILE)).astype(jnp.int32)


def _moe_meta(counts, tril_tiles, upper, n_blocks):
    n_tiles = counts.shape[0]
    sds = jax.ShapeDtypeStruct
    src, dst, blk, used = pl.pallas_call(
        functools.partial(_moe_meta_kernel, n_blocks=n_blocks),
        out_shape=[sds((n_tiles, LANES), jnp.int32), sds((n_tiles, LANES), jnp.int32),
                   sds((n_blocks, 1), jnp.int32), sds((1, 1), jnp.int32)],
        name="moe_meta",
    )(counts.reshape(n_tiles, LANES), tril_tiles, upper)
    flat = lambda tab: tab[:, N_GROUPS:N_GROUPS + N_EXPERTS].reshape(n_tiles * N_EXPERTS)
    return (flat(counts.reshape(n_tiles, LANES)), flat(src), flat(dst)), blk.reshape(n_blocks), used.reshape(1)


RUN_BITS = SORT_ROWS.bit_length()


def _run_copies(n, src, src_row, dst, dst_row, sem, fn):
    for b in range(RUN_BITS - 1, RUN_ALIGN.bit_length() - 2, -1):
        size = 1 << b
        done = lax.shift_left(lax.shift_right_logical(n, b + 1), b + 1)

        @pl.when((n & size) != 0)
        def _():
            fn(pltpu.make_async_copy(src.at[pl.ds(pl.multiple_of(src_row + done, RUN_ALIGN), size), :],
                                     dst.at[pl.ds(pl.multiple_of(dst_row + done, RUN_ALIGN), size), :], sem))


def _tile_runs(cnt_ref, src_ref, dst_ref, tile, sorted_ref, slots_ref, sem, fn, to_slots, sorted_base=0):
    for e in range(N_EXPERTS):
        k = tile * N_EXPERTS + e
        in_sorted = src_ref[k] - sorted_base
        if to_slots:
            _run_copies(cnt_ref[k], sorted_ref, in_sorted, slots_ref, dst_ref[k], sem, fn)
        else:
            _run_copies(cnt_ref[k], slots_ref, dst_ref[k], sorted_ref, in_sorted, sem, fn)


def _to_slots_kernel(cnt_ref, src_ref, dst_ref, hs_ref, zero_ref, xs_ref, sem):
    del zero_ref
    t = pl.program_id(0)
    runs = functools.partial(_tile_runs, cnt_ref, src_ref, dst_ref, tile=t, sorted_ref=hs_ref.at[0],
                             slots_ref=xs_ref, sem=sem, to_slots=True, sorted_base=t * SORT_ROWS)
    runs(fn=lambda cp: cp.start())
    runs(fn=lambda cp: cp.wait())


def _to_slots(tables, hs_sorted, n_slots):
    n_tiles = hs_sorted.shape[0]
    return pl.pallas_call(
        _to_slots_kernel,
        out_shape=jax.ShapeDtypeStruct((n_slots, HALF_D), jnp.int32),
        grid_spec=pltpu.PrefetchScalarGridSpec(
            num_scalar_prefetch=3,
            grid=(n_tiles,),
            in_specs=[pl.BlockSpec((1, SORT_ROWS, HALF_D), lambda t, *_: (t, 0, 0)),
                      pl.BlockSpec(memory_space=pl.ANY)],
            out_specs=pl.BlockSpec(memory_space=pl.ANY),
            scratch_shapes=[pltpu.SemaphoreType.DMA(())]),
        input_output_aliases={4: 0},
        compiler_params=_cparams(("arbitrary",)),
        name="moe_to_slots",
    )(*tables, hs_sorted, jnp.zeros((n_slots, HALF_D), jnp.int32))


def _expert_kernel(be_ref, nb_ref, x_ref, wgu_ref, wd_ref, y_ref):
    i = pl.program_id(0)

    @pl.when(i < nb_ref[0])
    def _():
        lo, hi = _unpack_pairs(x_ref[...])
        gu = _dot(lo, wgu_ref[0, 0:HALF_D, :]) + _dot(hi, wgu_ref[0, HALF_D:, :])
        hid = _silu(gu[:, 0:D_EXPERT]) * gu[:, D_EXPERT:]
        y_ref[...] = _pack_pairs(_dot(hid.astype(BF16), wd_ref[0]))

    @pl.when(i >= nb_ref[0])
    def _():
        y_ref[...] = jnp.zeros_like(y_ref)


def _experts(blk_expert, n_used, xs, w_gu, w_down):
    n_slots = xs.shape[0]
    return pl.pallas_call(
        _expert_kernel,
        out_shape=jax.ShapeDtypeStruct((n_slots, HALF_D), jnp.int32),
        grid_spec=pltpu.PrefetchScalarGridSpec(
            num_scalar_prefetch=2,
            grid=(n_slots // EXPERT_TILE,),
            in_specs=[pl.BlockSpec((EXPERT_TILE, HALF_D), lambda i, be, nb: (i, 0)),
                      pl.BlockSpec((1, D_MODEL, 2 * D_EXPERT), lambda i, be, nb: (be[i], 0, 0)),
                      pl.BlockSpec((1, D_EXPERT, D_MODEL), lambda i, be, nb: (be[i], 0, 0))],
            out_specs=pl.BlockSpec((EXPERT_TILE, HALF_D), lambda i, be, nb: (i, 0))),
        compiler_params=_cparams(("arbitrary",)),
        name="moe_experts",
    )(blk_expert, n_used, xs, w_gu, w_down)


def _combine_kernel(cnt_ref, src_ref, dst_ref, x_ref, route_ref, mb_ref, mc_ref, ys_ref, xo_ref, buf, sem,
                    *, n_lat):
    tiles = pl.num_programs(1)
    t = pl.program_id(0) * tiles + pl.program_id(1)
    buf[...] = jnp.zeros_like(buf)
    runs = functools.partial(_tile_runs, cnt_ref, src_ref, dst_ref, tile=t, sorted_ref=buf, slots_ref=ys_ref,
                             sem=sem, to_slots=False, sorted_base=t * SORT_ROWS)
    runs(fn=lambda cp: cp.start())
    route = route_ref[0]
    col = lax.broadcasted_iota(jnp.int32, (1, SORT_ROWS), 1).astype(F32)
    weights = (jnp.where(col == route[:, ROUTE_P0:ROUTE_P0 + 1], route[:, ROUTE_G0:ROUTE_G0 + 1], 0.0)
               + jnp.where(col == route[:, ROUTE_P1:ROUTE_P1 + 1], route[:, ROUTE_G1:ROUTE_G1 + 1], 0.0)
               ).astype(BF16)
    runs(fn=lambda cp: cp.wait())
    lo, hi = _unpack_pairs(buf[...])
    moe = jnp.concatenate([_dot(weights, lo), _dot(weights, hi)], 1)
    is_ctx = _is_ctx_rows(pl.program_id(1), TOKEN_TILE, n_lat)
    xo_ref[0] = x_ref[0] + _mod_vec(mb_ref, mc_ref, 5, is_ctx) * moe


def _combine(x, route, mod, tables, ys, n_lat):
    n_batch, s, d = x.shape
    tiles = s // TOKEN_TILE
    tok = lambda width: pl.BlockSpec((1, TOKEN_TILE, width), lambda b, j, *_: (b, j, 0))
    mod_specs = [pl.BlockSpec((1, 1, N_MOD * d), lambda b, j, *_: (b, 0, 0)),
                 pl.BlockSpec((1, 1, N_MOD * d), lambda b, j, *_: (n_batch, 0, 0))]
    return pl.pallas_call(
        functools.partial(_combine_kernel, n_lat=n_lat),
        out_shape=jax.ShapeDtypeStruct((n_batch, s, d), F32),
        grid_spec=pltpu.PrefetchScalarGridSpec(
            num_scalar_prefetch=3,
            grid=(n_batch, tiles),
            in_specs=[tok(d), tok(LANES)] + mod_specs + [pl.BlockSpec(memory_space=pl.ANY)],
            out_specs=tok(d),
            scratch_shapes=[pltpu.VMEM((SORT_ROWS, HALF_D), jnp.int32), pltpu.SemaphoreType.DMA(())]),
        input_output_aliases={3: 0},
        compiler_params=_cparams(("arbitrary", "arbitrary")),
        name="moe_combine",
    )(*tables, x, route, mod, mod, ys)


def _final_kernel(x_ref, g_ref, o_ref):
    x = x_ref[0]
    o_ref[0] = x * lax.rsqrt(jnp.mean(x * x, -1, keepdims=True) + EPS) * g_ref[...]


def _final_norm(x, g, n_lat):
    n_batch, _, d = x.shape
    tm = 512
    return pl.pallas_call(
        _final_kernel,
        out_shape=jax.ShapeDtypeStruct((n_batch, n_lat, d), F32),
        grid=(n_batch, n_lat // tm),
        in_specs=[pl.BlockSpec((1, tm, d), lambda b, j: (b, j, 0)), pl.BlockSpec((1, d), lambda b, j: (0, 0))],
        out_specs=pl.BlockSpec((1, tm, d), lambda b, j: (b, j, 0)),
        compiler_params=_cparams(("parallel", "arbitrary")),
        name="final_norm",
    )(x, g.reshape(1, d))


def _block_diag(w):
    heads, di, dj = w.shape
    eye = jnp.eye(heads, dtype=w.dtype)
    return (eye[:, None, :, None] * w[:, :, None, :]).reshape(heads * di, heads * dj)


def _rope_tables(n_lat):
    rows = n_lat // GRID_W
    row = np.repeat(np.arange(rows, dtype=np.float64), GRID_W)
    col = np.tile(np.arange(GRID_W, dtype=np.float64), rows)
    inv = ROPE_BASE ** (-np.arange(ROPE_FREQS, dtype=np.float64) / ROPE_FREQS)
    ang = np.concatenate([row[:, None] * inv, col[:, None] * inv], -1)
    cos = np.tile(np.concatenate([np.cos(ang), np.cos(ang)], -1), (1, W_MIX // DIFF_DK))
    sin = np.tile(np.concatenate([-np.sin(ang), np.sin(ang)], -1), (1, W_MIX // DIFF_DK))
    return jnp.asarray(cos, F32), jnp.asarray(sin, F32)


def kernel(x, c, ctx, c_ctx, w_ada, b_ada, g_mix, g_ffn, w_in, w_out, a_conv_w, a_conv_b, a_w_r, a_b_r,
           a_w_i, a_b_i, a_lam, b_lq1, b_lk1, b_lq2, b_lk2, b_sub_g, c_conv_w, c_conv_b, c_ln_g, c_ln_b,
           c_w_pw, c_b_pw, d_conv_w, d_conv_b, d_w_f1, d_b_f1, d_freq, d_w_f2, d_b_f2, d_w_f3, d_decay,
           d_bias, moe_w_rg, moe_b_rg, moe_w_re, moe_b_re, moe_w_gate, moe_w_up, moe_w_down, g_final):
    n_batch, n_lat, d = x.shape
    n_ctx = ctx.shape[1]
    depth = w_ada.shape[0]
    s = n_lat + n_ctx
    assert d == D_MODEL and s % TOKEN_TILE == 0 and n_lat % CONV_BLOCK == 0 and n_ctx <= CONV_BLOCK
    assert n_lat % Q_TILE == 0 and n_ctx % LANES == 0 and n_lat % GRID_W == 0

    xs = jnp.concatenate([x, ctx], 1)
    mod_rows = -(-(n_batch + 1) // MOD_ROWS_PAD) * MOD_ROWS_PAD
    c_all = jnp.concatenate([c, c_ctx[None], jnp.zeros((mod_rows - n_batch - 1, d), F32)], 0)
    mod_all = _ada_table(c_all, w_ada, b_ada)

    rope_cos, rope_sin = _rope_tables(n_lat)
    fwd_np, inv_np, filt_np = _dft_matrices()
    fwd_m, inv_m, filt_m = (jnp.asarray(m, F32).astype(BF16) for m in (fwd_np, inv_np, filt_np))
    feats_l, aux_l = (jnp.asarray(m) for m in _hyena_positions(2 * n_lat, n_lat, n_lat))
    feats_c, aux_c = (jnp.asarray(m) for m in _hyena_positions(CONV_FFT, CONV_BLOCK, n_ctx))
    gmat = jnp.asarray(np.kron(np.eye(DIFF_HEADS), np.full((DIFF_DV, DIFF_DV), 1.0 / DIFF_DV)), F32).astype(BF16)
    tri = jnp.asarray(np.tril(np.ones((TOKEN_TILE, TOKEN_TILE)), -1), F32).astype(BF16)
    n_tiles = n_batch * (s // TOKEN_TILE)
    tril_tiles = jnp.asarray(np.tril(np.ones((n_tiles, n_tiles)), -1), F32).astype(BF16)
    upper = jnp.asarray(np.triu(np.ones((LANES, LANES)), 1), F32).astype(BF16)

    w1 = jnp.pad(d_w_f1, ((0, 0), (0, LANES - HYENA_EMB), (0, 0)))
    w3 = d_w_f3.reshape(depth, HYENA_FFN, HYENA_ORDER, 2 * W_MIX).transpose(0, 2, 1, 3)
    dec = d_decay.reshape(depth, HYENA_ORDER, 1, 2 * W_MIX)
    row = lambda v: v.reshape(depth, 1, -1)
    hy = _hyena_filters(feats_l, aux_l, feats_c, aux_c, w1, row(d_b_f1), row(d_freq), d_w_f2, row(d_b_f2),
                        w3, dec, filt_m)

    n_tok = n_batch * s
    for l in range(depth):
        last = l == depth - 1
        lam_init = 0.8 - 0.6 * math.exp(-0.3 * l)
        mod = mod_all[l].reshape(mod_rows, 1, N_MOD * d)
        p_a, p_b, p_c, p_d = _in_proj(xs, mod, g_mix[l], w_in[l].astype(BF16), n_lat)

        w_gates = jnp.stack([jnp.concatenate([_block_diag(a_w_r[l, dr]), _block_diag(a_w_i[l, dr])], 1)
                             for dr in range(2)]).astype(BF16)
        b_gates = jnp.concatenate([a_b_r[l], a_b_i[l]], -1)[:, None, :]
        y_a = _rglru(p_a, a_conv_w[l], a_conv_b[l][:, None, :], w_gates, b_gates, a_lam[l][:, None, :], n_lat)
        lqk = jnp.stack([b_lq1[l], b_lk1[l], b_lq2[l], b_lk2[l]])
        y_b = _diff_attn(p_b, rope_cos, rope_sin, lqk, jnp.tile(b_sub_g[l], DIFF_HEADS)[None], gmat,
                         n_lat, lam_init)
        y_c = _conformer(p_c, c_conv_w[l], c_conv_b[l][None], c_ln_g[l][None], c_ln_b[l][None],
                         c_w_pw[l].astype(BF16), c_b_pw[l][None], n_lat)
        y_d = _hyena(p_d, d_conv_w[l], d_conv_b[l][None], d_bias[l][:, None, :], [f[l] for f in hy],
                     fwd_m, inv_m, n_lat)

        w_route = jnp.pad(jnp.concatenate([moe_w_rg[l], moe_w_re[l]], 1),
                          ((0, 0), (0, LANES - N_GROUPS - N_EXPERTS)))
        b_route = jnp.pad(jnp.concatenate([moe_b_rg[l], moe_b_re[l]]), (0, LANES - N_GROUPS - N_EXPERTS))[None]
        xs, hs_sorted, route, counts = _out_route(xs, (y_a, y_b, y_c, y_d), mod, g_ffn[l], w_out[l].astype(BF16),
                                                  w_route, b_route, tri, upper, n_lat, not last)

        n_assign = 2 * (n_tok if not last else n_batch * n_lat)
        n_rows_max = n_assign + n_tiles * N_EXPERTS * (RUN_ALIGN - 1)
        n_slots = -(-n_rows_max // EXPERT_TILE) * EXPERT_TILE + N_EXPERTS * EXPERT_TILE
        tables, blk_expert, n_used = _moe_meta(counts, tril_tiles, upper, n_slots // EXPERT_TILE)
        slots = _to_slots(tables, hs_sorted, n_slots)
        w_gu = jnp.concatenate([moe_w_gate[l], moe_w_up[l]], -1).astype(BF16)
        ys = _experts(blk_expert, n_used, slots, w_gu, moe_w_down[l].astype(BF16))
        xs = _combine(xs, route, mod, tables, ys, n_lat)

    return _final_norm(xs, g_final, n_lat)
```

```python
import functools
import math

import numpy as np
import jax
import jax.numpy as jnp
from jax import lax
from jax.experimental import pallas as pl
from jax.experimental.pallas import tpu as pltpu

F32 = jnp.float32
BF16 = jnp.bfloat16

D_MODEL = 1024
N_MOD = 6
EPS = 1e-6
W_MIX = 256
LRU_HEADS = 4
LRU_CONV = 4
RG_C = 8.0
DIFF_HEADS = 4
DIFF_DV = W_MIX // DIFF_HEADS
DIFF_DK = DIFF_DV // 2
ROPE_BASE = 10000.0
ROPE_FREQS = DIFF_DK // 4
GRID_W = 64
CONF_K = 31
HYENA_ORDER = 2
HYENA_SHORT = 3
HYENA_EMB = 33
HYENA_BANDS = (HYENA_EMB - 1) // 2
HYENA_FFN = 64
N_GROUPS = 4
EXPERTS_PER_GROUP = 8
N_EXPERTS = N_GROUPS * EXPERTS_PER_GROUP
D_EXPERT = 512
IN_A, IN_B, IN_C, IN_D = 2 * W_MIX, 3 * W_MIX, 2 * W_MIX, 3 * W_MIX
D_IN = IN_A + IN_B + IN_C + IN_D

LANES = 128
TOKEN_TILE = 768
Q_TILE = 512
KEY_CHUNK = 256
CONV_BLOCK = 512
CONV_FFT = 2 * CONV_BLOCK
EXPERT_TILE = 512
MOD_ROWS_PAD = 8
VMEM_LIMIT = 56 * 1024 * 1024


def _cparams(sem, vmem=VMEM_LIMIT):
    return pltpu.CompilerParams(dimension_semantics=sem, vmem_limit_bytes=vmem)


def _dot(a, b):
    return jnp.dot(a, b, preferred_element_type=F32)


def _split_bf16(a):
    hi = a.astype(BF16)
    lo = (a - hi.astype(F32)).astype(BF16)
    return hi, lo


def _dot3(a, b):
    ah, al = _split_bf16(a)
    bh, bl = _split_bf16(b)
    return _dot(ah, bh) + _dot(al, bh) + _dot(ah, bl)


def _dot2_lhs(a, b_bf16):
    ah, al = _split_bf16(a)
    return _dot(ah, b_bf16) + _dot(al, b_bf16)


def _dot2_rhs(a_bf16, b):
    bh, bl = _split_bf16(b)
    return _dot(a_bf16, bh) + _dot(a_bf16, bl)


def _sigmoid(x):
    return 1.0 / (1.0 + jnp.exp(-x))


def _silu(x):
    return x * _sigmoid(x)


def _gelu_tanh(x):
    return 0.5 * x * (1.0 + jnp.tanh(math.sqrt(2.0 / math.pi) * (x + 0.044715 * (x * x * x))))


def _is_ctx_rows(tile_idx, tile_rows, n_lat):
    rows = tile_idx * tile_rows + lax.broadcasted_iota(jnp.int32, (tile_rows, 1), 0)
    return rows >= n_lat


def _mod_vec(mb_ref, mc_ref, k, is_ctx):
    vb = mb_ref[0, :, k * D_MODEL:(k + 1) * D_MODEL]
    vc = mc_ref[0, :, k * D_MODEL:(k + 1) * D_MODEL]
    return jnp.where(is_ctx, vc, vb)


def _rms_modulate(x, g, shift, scale):
    y = x * lax.rsqrt(jnp.mean(x * x, -1, keepdims=True) + EPS)
    return (y * g) * (1.0 + scale) + shift


def _ada_kernel(c_ref, w_ref, b_ref, o_ref):
    o_ref[0] = _dot3(_silu(c_ref[...]), w_ref[0]) + b_ref[0]


def _ada_table(c_all, w_ada, b_ada):
    depth, d, n = w_ada.shape
    rows = c_all.shape[0]
    tn = 1536
    return pl.pallas_call(
        _ada_kernel,
        out_shape=jax.ShapeDtypeStruct((depth, rows, n), F32),
        grid=(depth, n // tn),
        in_specs=[pl.BlockSpec((rows, d), lambda l, j: (0, 0)),
                  pl.BlockSpec((1, d, tn), lambda l, j: (l, 0, j)),
                  pl.BlockSpec((1, 1, tn), lambda l, j: (l, 0, j))],
        out_specs=pl.BlockSpec((1, rows, tn), lambda l, j: (l, 0, j)),
        compiler_params=_cparams(("arbitrary", "arbitrary")),
        name="ada_table",
    )(c_all, w_ada, b_ada.reshape(depth, 1, n))


def _in_proj_kernel(x_ref, mb_ref, mc_ref, g_ref, w_ref, pa_ref, pb_ref, pc_ref, pd_ref, *, n_lat):
    is_ctx = _is_ctx_rows(pl.program_id(1), TOKEN_TILE, n_lat)
    h = _rms_modulate(x_ref[0], g_ref[...], _mod_vec(mb_ref, mc_ref, 0, is_ctx),
                      _mod_vec(mb_ref, mc_ref, 1, is_ctx)).astype(BF16)
    col = 0
    for ref, width in ((pa_ref, IN_A), (pb_ref, IN_B), (pc_ref, IN_C), (pd_ref, IN_D)):
        ref[0] = _dot(h, w_ref[:, col:col + width]).astype(BF16)
        col += width


def _mod_specs(n_batch):
    width = N_MOD * D_MODEL
    return [pl.BlockSpec((1, 1, width), lambda b, j: (b, 0, 0)),
            pl.BlockSpec((1, 1, width), lambda b, j: (n_batch, 0, 0))]


def _in_proj(x, mod, g, w, n_lat):
    n_batch, s, d = x.shape
    widths = (IN_A, IN_B, IN_C, IN_D)
    return pl.pallas_call(
        functools.partial(_in_proj_kernel, n_lat=n_lat),
        out_shape=[jax.ShapeDtypeStruct((n_batch, s, wd), BF16) for wd in widths],
        grid=(n_batch, s // TOKEN_TILE),
        in_specs=[pl.BlockSpec((1, TOKEN_TILE, d), lambda b, j: (b, j, 0))] + _mod_specs(n_batch) + [
            pl.BlockSpec((1, d), lambda b, j: (0, 0)),
            pl.BlockSpec((d, D_IN), lambda b, j: (0, 0))],
        out_specs=[pl.BlockSpec((1, TOKEN_TILE, wd), lambda b, j: (b, j, 0)) for wd in widths],
        compiler_params=_cparams(("parallel", "arbitrary")),
        name="in_proj",
    )(x, mod, mod, g.reshape(1, d), w)


SUBLANES = 8


def _group_scan(a, b, reverse):
    n, w = a.shape
    a = a.reshape(n // SUBLANES, SUBLANES, w)
    b = b.reshape(n // SUBLANES, SUBLANES, w)
    row = lax.broadcasted_iota(jnp.int32, (1, SUBLANES, 1), 1)
    d = 1
    while d < SUBLANES:
        keep = (row < SUBLANES - d) if reverse else (row >= d)
        shift = SUBLANES - d if reverse else d
        b = a * jnp.where(keep, pltpu.roll(b, shift, 1), 0.0) + b
        a = a * jnp.where(keep, pltpu.roll(a, shift, 1), 1.0)
        d *= 2
    return a.reshape(n, w), b.reshape(n, w)


def _chain_groups(a_ref, b_ref, h_ref, row0, n, h0, reverse):
    groups = n // SUBLANES

    def body(g, carry):
        gg = (groups - 1 - g) if reverse else g
        rows = pl.ds(pl.multiple_of(row0 + gg * SUBLANES, SUBLANES), SUBLANES)
        h = b_ref[rows, :] + a_ref[rows, :] * carry
        h_ref[rows, :] = h
        return h[0:1] if reverse else h[SUBLANES - 1:SUBLANES]

    return lax.fori_loop(0, groups, body, h0)


def _rglru_kernel(p_ref, cw_ref, cb_ref, wg_ref, bg_ref, lam_ref, o_ref,
                  xpad, a_s, b_s, hf_s, hb_s, *, n_lat, n_ctx):
    pad = 8
    lat0, ctx0 = pad, 2 * pad + n_lat
    zeros = jnp.zeros((pad, W_MIX), F32)
    xpad[0:pad, :] = zeros
    xpad[lat0:lat0 + n_lat, :] = p_ref[0, 0:n_lat, 0:W_MIX].astype(F32)
    xpad[lat0 + n_lat:ctx0, :] = zeros
    xpad[ctx0:ctx0 + n_ctx, :] = p_ref[0, n_lat:n_lat + n_ctx, 0:W_MIX].astype(F32)
    xpad[ctx0 + n_ctx:ctx0 + n_ctx + pad, :] = zeros

    for d, reverse in enumerate((False, True)):
        def conv(off, n):
            acc = cb_ref[d]
            for k in range(LRU_CONV):
                sh = k if reverse else k - (LRU_CONV - 1)
                acc = acc + cw_ref[d, k:k + 1, :] * xpad[off + sh:off + sh + n, :]
            return acc

        u = jnp.concatenate([conv(lat0, n_lat), conv(ctx0, n_ctx)], 0)
        gates = _dot(u.astype(BF16), wg_ref[d]) + bg_ref[d]
        r = _sigmoid(gates[:, 0:W_MIX])
        i = _sigmoid(gates[:, W_MIX:2 * W_MIX])
        z = -lam_ref[d]
        softplus = jnp.maximum(z, 0.0) + jnp.log(1.0 + jnp.exp(-jnp.abs(z)))
        a = jnp.exp(-RG_C * r * softplus)
        b = jnp.sqrt(1.0 - a * a) * (i * u)
        a_s[...], b_s[...] = _group_scan(a, b, reverse)
        h_out = hb_s if reverse else hf_s
        h_ctx = _chain_groups(a_s, b_s, h_out, n_lat, n_ctx, jnp.zeros((1, W_MIX), F32), reverse)
        _chain_groups(a_s, b_s, h_out, 0, n_lat, h_ctx, reverse)

    xg = p_ref[0, :, W_MIX:2 * W_MIX].astype(F32)
    o_ref[0] = ((hf_s[...] + hb_s[...]) * _gelu_tanh(xg)).astype(BF16)


def _rglru(p_a, conv_w, conv_b, w_gates, b_gates, lam, n_lat):
    n_batch, s, _ = p_a.shape
    n_ctx = s - n_lat
    full = lambda shape: pl.BlockSpec(shape, lambda b: (0,) * len(shape))
    return pl.pallas_call(
        functools.partial(_rglru_kernel, n_lat=n_lat, n_ctx=n_ctx),
        out_shape=jax.ShapeDtypeStruct((n_batch, s, W_MIX), BF16),
        grid=(n_batch,),
        in_specs=[pl.BlockSpec((1, s, IN_A), lambda b: (b, 0, 0)),
                  full((2, LRU_CONV, W_MIX)), full((2, 1, W_MIX)),
                  full((2, W_MIX, 2 * W_MIX)), full((2, 1, 2 * W_MIX)), full((2, 1, W_MIX))],
        out_specs=pl.BlockSpec((1, s, W_MIX), lambda b: (b, 0, 0)),
        scratch_shapes=[pltpu.VMEM((s + 24, W_MIX), F32)] + [pltpu.VMEM((s, W_MIX), F32)] * 4,
        compiler_params=_cparams(("parallel",)),
        name="rglru",
    )(p_a, conv_w, conv_b, w_gates, b_gates, lam)


def _diff_attn_kernel(p_ref, cos_ref, sin_ref, lqk_ref, subg_ref, gmat_ref, o_ref, q_s, k_s, s_a, s_b, e_buf,
                      *, n_lat, n_ctx, lam_init):
    lane = lax.broadcasted_iota(jnp.int32, (1, W_MIX), 1)
    first_half = (lane % DIFF_DK) < (DIFF_DK // 2)

    def rope(x):
        partner = jnp.where(first_half, pltpu.roll(x, W_MIX - DIFF_DK // 2, 1),
                            pltpu.roll(x, DIFF_DK // 2, 1))
        return x * cos_ref[...] + partner * sin_ref[...]

    scale = DIFF_DK ** -0.5 * math.log2(math.e)
    q = p_ref[0, :, 0:W_MIX].astype(F32)
    k = p_ref[0, :, W_MIX:2 * W_MIX].astype(F32)
    q = (jnp.concatenate([rope(q[0:n_lat]), q[n_lat:]], 0) * scale).astype(BF16)
    for g in range(2 * DIFF_HEADS):
        q_s[g] = q[:, g * DIFF_DK:(g + 1) * DIFF_DK]
    k_s[...] = jnp.concatenate([rope(k[0:n_lat]), k[n_lat:]], 0).T.astype(BF16)

    lqk = lqk_ref[...]
    lam = (jnp.exp(jnp.sum(lqk[0:1] * lqk[1:2], -1, keepdims=True))
           - jnp.exp(jnp.sum(lqk[2:3] * lqk[3:4], -1, keepdims=True)) + lam_init)

    n_stages = 2 * DIFF_HEADS
    s_bufs = (s_a, s_b)

    def attend(q0, nq, kv0, nk):
        chunks = [(c0, min(KEY_CHUNK, nk - c0)) for c0 in range(0, nk, KEY_CHUNK)]

        def scores(g, c0, cw, top):
            s = _dot(q_s[g, pl.ds(q0, nq), :], k_s[g * DIFF_DK:(g + 1) * DIFF_DK, kv0 + c0:kv0 + c0 + cw])
            s_bufs[g % 2][0:nq, c0:c0 + cw] = s
            return s if top is None else jnp.maximum(top, s)

        def probs(g, c0, cw, m, tot):
            e = jnp.exp2(s_bufs[g % 2][0:nq, c0:c0 + cw] - m)
            e_buf[g % 4, 0:nq, c0:c0 + cw] = e.astype(BF16)
            return e if tot is None else tot + e

        def weighted(head, c0, cw, out):
            h, r0, r1 = head
            b = 2 * (h % 2)
            w = e_buf[b, 0:nq, c0:c0 + cw] * r0 - e_buf[b + 1, 0:nq, c0:c0 + cw] * r1
            part = _dot(w, p_ref[0, kv0 + c0:kv0 + c0 + cw, 2 * W_MIX:3 * W_MIX])
            return part if out is None else out + part

        top = None
        for c0, cw in chunks:
            top = scores(0, c0, cw, top)
        m = jnp.max(top, -1, keepdims=True)
        acc = jnp.zeros((nq, W_MIX), F32)
        head, r_first = None, None
        for g in range(n_stages + 1):
            top, tot, out = None, None, None
            for c0, cw in chunks:
                if g + 1 < n_stages:
                    top = scores(g + 1, c0, cw, top)
                if g < n_stages:
                    tot = probs(g, c0, cw, m, tot)
                if head is not None:
                    out = weighted(head, c0, cw, out)
            if head is not None:
                acc = jnp.where(lane // DIFF_DV == head[0], out, acc)
                head = None
            if g < n_stages:
                r = 1.0 / jnp.sum(tot, -1, keepdims=True)
                if g % 2 == 0:
                    r_first = r
                else:
                    head = (g // 2, r_first.astype(BF16), (lam * r).astype(BF16))
            if g + 1 < n_stages:
                m = jnp.max(top, -1, keepdims=True)
        ms = _dot2_lhs(acc * acc, gmat_ref[...])
        y = acc * lax.rsqrt(ms + EPS) * subg_ref[...] * (1.0 - lam_init)
        o_ref[0, pl.ds(q0, nq), :] = y.astype(BF16)

    def lat_tile(i, carry):
        attend(pl.multiple_of(i * Q_TILE, Q_TILE), Q_TILE, 0, n_lat + n_ctx)
        return carry

    lax.fori_loop(0, n_lat // Q_TILE, lat_tile, 0)
    attend(n_lat, n_ctx, n_lat, n_ctx)


def _diff_attn(p_b, rope_cos, rope_sin, lqk, sub_g, gmat, n_lat, lam_init):
    n_batch, s, _ = p_b.shape
    n_ctx = s - n_lat
    full = lambda shape: pl.BlockSpec(shape, lambda b: (0,) * len(shape))
    return pl.pallas_call(
        functools.partial(_diff_attn_kernel, n_lat=n_lat, n_ctx=n_ctx, lam_init=lam_init),
        out_shape=jax.ShapeDtypeStruct((n_batch, s, W_MIX), BF16),
        grid=(n_batch,),
        in_specs=[pl.BlockSpec((1, s, IN_B), lambda b: (b, 0, 0)),
                  full((n_lat, W_MIX)), full((n_lat, W_MIX)), full((4, DIFF_DK)),
                  full((1, W_MIX)), full((W_MIX, W_MIX))],
        out_specs=pl.BlockSpec((1, s, W_MIX), lambda b: (b, 0, 0)),
        scratch_shapes=[pltpu.VMEM((2 * DIFF_HEADS, s, DIFF_DK), BF16), pltpu.VMEM((W_MIX, s), BF16),
                        pltpu.VMEM((Q_TILE, s), F32), pltpu.VMEM((Q_TILE, s), F32),
                        pltpu.VMEM((4, Q_TILE, s), BF16)],
        compiler_params=_cparams(("parallel",)),
        name="diff_attn",
    )(p_b, rope_cos, rope_sin, lqk, sub_g, gmat)


def _conformer_kernel(p_ref, cw_ref, cb_ref, lng_ref, lnb_ref, wpw_ref, bpw_ref, o_ref, upad,
                      *, n_lat, n_ctx):
    pad = 16
    half = CONF_K // 2
    lat0, ctx0 = pad, 2 * pad + n_lat
    val = p_ref[0, :, 0:W_MIX].astype(F32)
    gate = p_ref[0, :, W_MIX:2 * W_MIX].astype(F32)
    u = val * _sigmoid(gate)
    zeros = jnp.zeros((pad, W_MIX), F32)
    upad[0:pad, :] = zeros
    upad[lat0:lat0 + n_lat, :] = u[0:n_lat]
    upad[lat0 + n_lat:ctx0, :] = zeros
    upad[ctx0:ctx0 + n_ctx, :] = u[n_lat:]
    upad[ctx0 + n_ctx:ctx0 + n_ctx + pad, :] = zeros

    def conv(off, n):
        acc = cb_ref[...]
        for r in range(SUBLANES):
            group = None
            for k in range(CONF_K):
                if (k - half) % SUBLANES == r:
                    base = off + (k - half) - r
                    term = cw_ref[k:k + 1, :] * upad[base:base + n + SUBLANES, :]
                    group = term if group is None else group + term
            acc = acc + group[r:r + n]
        return acc

    y = jnp.concatenate([conv(lat0, n_lat), conv(ctx0, n_ctx)], 0)
    mu = jnp.mean(y, -1, keepdims=True)
    yc = y - mu
    var = jnp.mean(yc * yc, -1, keepdims=True)
    y = yc * lax.rsqrt(var + 1e-5) * lng_ref[...] + lnb_ref[...]
    o_ref[0] = (_dot(_silu(y).astype(BF16), wpw_ref[...]) + bpw_ref[...]).astype(BF16)


def _conformer(p_c, conv_w, conv_b, ln_g, ln_b, w_pw, b_pw, n_lat):
    n_batch, s, _ = p_c.shape
    full = lambda shape: pl.BlockSpec(shape, lambda b: (0,) * len(shape))
    return pl.pallas_call(
        functools.partial(_conformer_kernel, n_lat=n_lat, n_ctx=s - n_lat),
        out_shape=jax.ShapeDtypeStruct((n_batch, s, W_MIX), BF16),
        grid=(n_batch,),
        in_specs=[pl.BlockSpec((1, s, IN_C), lambda b: (b, 0, 0)),
                  full((CONF_K, W_MIX)), full((1, W_MIX)), full((1, W_MIX)), full((1, W_MIX)),
                  full((W_MIX, W_MIX)), full((1, W_MIX))],
        out_specs=pl.BlockSpec((1, s, W_MIX), lambda b: (b, 0, 0)),
        scratch_shapes=[pltpu.VMEM((s + 48, W_MIX), F32)],
        compiler_params=_cparams(("parallel",)),
        name="conformer",
    )(p_c, conv_w, conv_b, ln_g, ln_b, w_pw, b_pw)


def _dft_matrices():
    n, hb = CONV_FFT, CONV_BLOCK
    k = np.arange(hb, dtype=np.float64)[:, None]
    t = np.arange(n, dtype=np.float64)[None, :]
    ang = 2.0 * np.pi * k * t / n
    full = np.concatenate([np.cos(ang), -np.sin(ang)], 0)
    full[hb, :] = np.cos(np.pi * t[0])
    fwd = full[:, :hb]
    wk = np.where(k == 0, 1.0, 2.0) / n
    tt = np.arange(hb, dtype=np.float64)[None, :]
    ang_i = 2.0 * np.pi * k * tt / n
    inv = np.concatenate([wk * np.cos(ang_i), -wk * np.sin(ang_i)], 0)
    inv[hb, :] = np.cos(np.pi * tt[0]) / n
    sign = np.cos(np.pi * k)
    filt = np.concatenate([sign * np.cos(ang), -sign * np.sin(ang)], 0)
    filt[hb, :] = np.cos(np.pi * t[0])
    filt[:, 0] = 0.0
    return fwd.astype(np.float32), inv.T.copy().astype(np.float32), filt.astype(np.float32)


def _hyena_positions(n_rows, center, seq_len):
    lag = np.arange(n_rows, dtype=np.float64) - center
    t = np.abs(lag)
    valid = t < seq_len
    t_unit = t / max(seq_len - 1, 1)
    bands = np.linspace(1e-4, HYENA_BANDS - 1, HYENA_BANDS)
    ang = (2.0 * np.pi / seq_len) * t[:, None] * bands[None, :]
    feats = np.zeros((n_rows, LANES), np.float64)
    feats[:, 0] = t_unit
    feats[:, 1:1 + HYENA_BANDS] = np.cos(ang)
    feats[:, 1 + HYENA_BANDS:HYENA_EMB] = -np.sin(ang)
    aux = np.zeros((n_rows, LANES), np.float64)
    aux[:, 0] = t_unit
    aux[:, 1] = valid
    aux[:, 2] = lag >= 0
    return feats.astype(np.float32), aux.astype(np.float32)


def _hyena_filter_kernel(fl_ref, al_ref, fc_ref, ac_ref, w1_ref, b1_ref, fr_ref, w2_ref, b2_ref,
                         w3_ref, dec_ref, filt_ref, a_lat, b_lat, n_lat, a_ctx, b_ctx, n_ctx):
    filt = filt_ref[...]
    row0 = lax.broadcasted_iota(jnp.int32, (CONV_BLOCK, 1), 0) == 0

    def taps(feats_ref, aux_ref):
        f = jnp.sin(fr_ref[0] * (_dot3(feats_ref[...], w1_ref[0]) + b1_ref[0]))
        f = jnp.sin(fr_ref[0] * (_dot3(f, w2_ref[0]) + b2_ref[0]))
        t_unit = aux_ref[:, 0:1]
        h = _dot3(f, w3_ref[0, 0]) * jnp.exp(-t_unit * jnp.abs(dec_ref[0, 0]))
        g = jnp.where(aux_ref[:, 2:3] > 0.5, h[:, 0:W_MIX], h[:, W_MIX:2 * W_MIX]) * aux_ref[:, 1:2]
        energy = jnp.sum(g * g, 0, keepdims=True)
        return g * lax.rsqrt(energy + EPS)

    def spectrum(window, a_out, b_out, n_out):
        hp = _dot2_rhs(filt, window)
        a_out[...] = hp[0:CONV_BLOCK]
        b_out[...] = jnp.where(row0, 0.0, hp[CONV_BLOCK:])
        n_out[...] = hp[CONV_BLOCK:CONV_BLOCK + 1]

    g = taps(fl_ref, al_ref)
    for dd in range(a_lat.shape[2]):
        spectrum(g[dd * CONV_BLOCK:dd * CONV_BLOCK + CONV_FFT],
                 a_lat.at[0, 0, dd], b_lat.at[0, 0, dd], n_lat.at[0, 0, dd])
    spectrum(taps(fc_ref, ac_ref), a_ctx.at[0, 0], b_ctx.at[0, 0], n_ctx.at[0, 0])


def _hyena_filters(feats_l, aux_l, feats_c, aux_c, w1, b1, freq, w2, b2, w3, decay, filt_m):
    depth = w1.shape[0]
    rows_l = feats_l.shape[0]
    n_off = rows_l // CONV_BLOCK - 1
    full = lambda shape: pl.BlockSpec(shape, lambda l, o: (0,) * len(shape))
    per_layer = lambda shape: pl.BlockSpec((1,) + shape, lambda l, o: (l,) + (0,) * len(shape))
    spec5 = lambda r: pl.BlockSpec((1, 1, n_off, r, W_MIX), lambda l, o: (l, o, 0, 0, 0))
    spec4 = lambda r: pl.BlockSpec((1, 1, r, W_MIX), lambda l, o: (l, o, 0, 0))
    sds = jax.ShapeDtypeStruct
    return pl.pallas_call(
        _hyena_filter_kernel,
        out_shape=[sds((depth, HYENA_ORDER, n_off, CONV_BLOCK, W_MIX), F32),
                   sds((depth, HYENA_ORDER, n_off, CONV_BLOCK, W_MIX), F32),
                   sds((depth, HYENA_ORDER, n_off, 1, W_MIX), F32),
                   sds((depth, HYENA_ORDER, CONV_BLOCK, W_MIX), F32),
                   sds((depth, HYENA_ORDER, CONV_BLOCK, W_MIX), F32),
                   sds((depth, HYENA_ORDER, 1, W_MIX), F32)],
        grid=(depth, HYENA_ORDER),
        in_specs=[full(feats_l.shape), full(aux_l.shape), full(feats_c.shape), full(aux_c.shape),
                  per_layer((LANES, HYENA_FFN)), per_layer((1, HYENA_FFN)), per_layer((1, HYENA_FFN)),
                  per_layer((HYENA_FFN, HYENA_FFN)), per_layer((1, HYENA_FFN)),
                  pl.BlockSpec((1, 1, HYENA_FFN, 2 * W_MIX), lambda l, o: (l, o, 0, 0)),
                  pl.BlockSpec((1, 1, 1, 2 * W_MIX), lambda l, o: (l, o, 0, 0)),
                  full(filt_m.shape)],
        out_specs=[spec5(CONV_BLOCK), spec5(CONV_BLOCK), spec5(1),
                   spec4(CONV_BLOCK), spec4(CONV_BLOCK), spec4(1)],
        compiler_params=_cparams(("arbitrary", "arbitrary")),
        name="hyena_filters",
    )(feats_l, aux_l, feats_c, aux_c, w1, b1, freq, w2, b2, w3, decay, filt_m)


def _hyena_kernel(p_ref, cw_ref, cb_ref, bias_ref, al_ref, bl_ref, nl_ref, ac_ref, bc_ref, nc_ref,
                  fwd_ref, inv_ref, o_ref, upad, *, n_lat, n_ctx):
    pad = 8
    lat0, ctx0 = pad, 2 * pad + n_lat
    n_blk = n_lat // CONV_BLOCK
    zeros = jnp.zeros((pad, IN_D), F32)
    upad[0:pad, :] = zeros
    upad[lat0:lat0 + n_lat, :] = p_ref[0, 0:n_lat, :].astype(F32)
    upad[lat0 + n_lat:ctx0, :] = zeros
    upad[ctx0:ctx0 + n_ctx, :] = p_ref[0, n_lat:n_lat + n_ctx, :].astype(F32)
    upad[ctx0 + n_ctx:ctx0 + n_ctx + pad, :] = zeros

    def conv(off, n):
        acc = cb_ref[...]
        for k in range(HYENA_SHORT):
            sh = k - HYENA_SHORT // 2
            acc = acc + cw_ref[k:k + 1, :] * upad[off + sh:off + sh + n, :]
        return acc

    u = jnp.concatenate([conv(lat0, n_lat), conv(ctx0, n_ctx)], 0)
    row0 = lax.broadcasted_iota(jnp.int32, (CONV_BLOCK, 1), 0) == 0
    fwd = fwd_ref[...]
    inv = inv_ref[...]

    def block_spectrum(zb):
        zf = _dot(fwd, zb.astype(BF16))
        return zf[0:CONV_BLOCK], zf[CONV_BLOCK:]

    def block_output(pre, pim, pnyq):
        pim = jnp.where(row0, pnyq, pim)
        return _dot(inv, jnp.concatenate([pre, pim], 0).astype(BF16))

    z = u[:, 0:W_MIX]
    for o in range(HYENA_ORDER):
        gate = u[:, (o + 1) * W_MIX:(o + 2) * W_MIX]
        spec = [block_spectrum(z[jb * CONV_BLOCK:(jb + 1) * CONV_BLOCK]) for jb in range(n_blk)]
        outs = []
        for ib in range(n_blk):
            pre = pim = pnyq = None
            for jb in range(n_blk):
                dd = ib - jb + n_blk - 1
                a, b, nq = al_ref[o, dd], bl_ref[o, dd], nl_ref[o, dd]
                zre, zim = spec[jb]
                t_re = zre * a - zim * b
                t_im = zre * b + zim * a
                t_ny = zim[0:1] * nq
                pre = t_re if pre is None else pre + t_re
                pim = t_im if pim is None else pim + t_im
                pnyq = t_ny if pnyq is None else pnyq + t_ny
            outs.append(block_output(pre, pim, pnyq))
        zc = jnp.concatenate([z[n_lat:], jnp.zeros((CONV_BLOCK - n_ctx, W_MIX), F32)], 0)
        zre, zim = block_spectrum(zc)
        a, b, nq = ac_ref[o], bc_ref[o], nc_ref[o]
        outs.append(block_output(zre * a - zim * b, zre * b + zim * a, zim[0:1] * nq)[0:n_ctx])
        y = jnp.concatenate(outs, 0)
        z = gate * (y + bias_ref[o] * z)
    o_ref[0] = z.astype(BF16)


def _hyena(p_d, conv_w, conv_b, bias, filters, fwd_m, inv_m, n_lat):
    n_batch, s, _ = p_d.shape
    n_off = 2 * (n_lat // CONV_BLOCK) - 1
    full = lambda shape: pl.BlockSpec(shape, lambda b: (0,) * len(shape))
    a_lat, b_lat, n_lat_f, a_ctx, b_ctx, n_ctx_f = filters
    return pl.pallas_call(
        functools.partial(_hyena_kernel, n_lat=n_lat, n_ctx=s - n_lat),
        out_shape=jax.ShapeDtypeStruct((n_batch, s, W_MIX), BF16),
        grid=(n_batch,),
        in_specs=[pl.BlockSpec((1, s, IN_D), lambda b: (b, 0, 0)),
                  full((HYENA_SHORT, IN_D)), full((1, IN_D)), full((HYENA_ORDER, 1, W_MIX)),
                  full((HYENA_ORDER, n_off, CONV_BLOCK, W_MIX)), full((HYENA_ORDER, n_off, CONV_BLOCK, W_MIX)),
                  full((HYENA_ORDER, n_off, 1, W_MIX)),
                  full((HYENA_ORDER, CONV_BLOCK, W_MIX)), full((HYENA_ORDER, CONV_BLOCK, W_MIX)),
                  full((HYENA_ORDER, 1, W_MIX)),
                  full((CONV_FFT, CONV_BLOCK)), full((CONV_BLOCK, CONV_FFT))],
        out_specs=pl.BlockSpec((1, s, W_MIX), lambda b: (b, 0, 0)),
        scratch_shapes=[pltpu.VMEM((s + 24, IN_D), F32)],
        compiler_params=_cparams(("parallel",)),
        name="hyena",
    )(p_d, conv_w, conv_b, bias, a_lat, b_lat, n_lat_f, a_ctx, b_ctx, n_ctx_f, fwd_m, inv_m)


ROUTE_P0, ROUTE_P1, ROUTE_G0, ROUTE_G1 = range(4)
RUN_ALIGN = 8
SORT_ROWS = 2 * TOKEN_TILE + N_EXPERTS * RUN_ALIGN
HALF_D = D_MODEL // 2


def _pack_pairs(v):
    bits = lax.bitcast_convert_type(v.astype(BF16).astype(F32), jnp.int32)
    return lax.shift_right_logical(bits[:, 0:HALF_D], 16) | bits[:, HALF_D:]


def _unpack_pairs(w):
    lo = lax.bitcast_convert_type(lax.shift_left(w, 16), F32).astype(BF16)
    hi = lax.bitcast_convert_type(w & jnp.int32(-65536), F32).astype(BF16)
    return lo, hi


def _out_route_kernel(x_ref, ya_ref, yb_ref, yc_ref, yd_ref, mb_ref, mc_ref, g_ref, wo_ref, wr_ref,
                      br_ref, tri_ref, upper_ref, xo_ref, hs_ref, route_ref, cnt_ref, *, n_lat, route_ctx):
    is_ctx = _is_ctx_rows(pl.program_id(1), TOKEN_TILE, n_lat)
    mix = None
    for g, ref in enumerate((ya_ref, yb_ref, yc_ref, yd_ref)):
        part = _dot(ref[0], wo_ref[g * W_MIX:(g + 1) * W_MIX, :])
        mix = part if mix is None else mix + part
    x = x_ref[0] + _mod_vec(mb_ref, mc_ref, 2, is_ctx) * mix
    xo_ref[0] = x
    h = _rms_modulate(x, g_ref[...], _mod_vec(mb_ref, mc_ref, 3, is_ctx), _mod_vec(mb_ref, mc_ref, 4, is_ctx))

    logits = _dot3(h, wr_ref[...]) + br_ref[...]
    lane = lax.broadcasted_iota(jnp.int32, (1, LANES), 1)
    neg = jnp.float32(-jnp.inf)

    def top1(vals):
        m = jnp.max(vals, -1, keepdims=True)
        idx = jnp.min(jnp.where(vals == m, lane, LANES), -1, keepdims=True)
        return m, idx

    grp = jnp.where(lane < N_GROUPS, logits, neg)
    m_g, g_idx = top1(grp)
    p_grp = 1.0 / jnp.sum(jnp.exp(grp - m_g), -1, keepdims=True)
    lo = N_GROUPS + g_idx * EXPERTS_PER_GROUP
    exp_l = jnp.where((lane >= lo) & (lane < lo + EXPERTS_PER_GROUP), logits, neg)
    m1, i1 = top1(exp_l)
    m2, i2 = top1(jnp.where(lane == i1, neg, exp_l))
    e2 = jnp.exp(m2 - m1)
    gate0 = p_grp / (1.0 + e2)
    gate1 = p_grp * e2 / (1.0 + e2)

    live = jnp.ones_like(is_ctx) if route_ctx else jnp.logical_not(is_ctx)
    oh0 = jnp.where((lane == i1) & live, 1.0, 0.0)
    oh1 = jnp.where((lane == i2) & live, 1.0, 0.0)
    both = jnp.concatenate([oh0, oh1], 1).astype(BF16)
    before = _dot(tri_ref[...], both)
    cnt0 = jnp.sum(oh0, 0, keepdims=True)
    cnt = cnt0 + jnp.sum(oh1, 0, keepdims=True)
    run = jnp.floor((cnt + (RUN_ALIGN - 1)) * (1.0 / RUN_ALIGN)) * RUN_ALIGN
    cnt_ref[0] = run.astype(jnp.int32)
    lower = _dot2_lhs(jnp.broadcast_to(run, (8, LANES)), upper_ref[...])[0:1]
    pos0 = jnp.sum((before[:, 0:LANES] + lower) * oh0, -1, keepdims=True)
    pos1 = jnp.sum((before[:, LANES:] + cnt0 + lower) * oh1, -1, keepdims=True)
    pos0 = jnp.where(live, pos0, -1.0)
    pos1 = jnp.where(live, pos1, -1.0)

    slab = jnp.zeros((TOKEN_TILE, LANES), F32)
    for col, v in ((ROUTE_P0, pos0), (ROUTE_P1, pos1), (ROUTE_G0, gate0), (ROUTE_G1, gate1)):
        slab = jnp.where(lane == col, v, slab)
    route_ref[0] = slab

    slab_t = slab.T
    q = lax.broadcasted_iota(jnp.int32, (SORT_ROWS, 1), 0).astype(F32)
    sel = (q == slab_t[ROUTE_P0:ROUTE_P0 + 1]) | (q == slab_t[ROUTE_P1:ROUTE_P1 + 1])
    sel = jnp.where(sel, 1.0, 0.0).astype(BF16)
    hs_ref[0] = _pack_pairs(_dot(sel, h.astype(BF16)))


def _out_route(x, ys, mod, g, w_out, w_route, b_route, tri, upper, n_lat, route_ctx):
    n_batch, s, d = x.shape
    tiles = s // TOKEN_TILE
    tok = lambda width: pl.BlockSpec((1, TOKEN_TILE, width), lambda b, j: (b, j, 0))
    full = lambda shape: pl.BlockSpec(shape, lambda b, j: (0,) * len(shape))
    per_tile = lambda rows, width: pl.BlockSpec((1, rows, width), lambda b, j: (b * tiles + j, 0, 0))
    sds = jax.ShapeDtypeStruct
    return pl.pallas_call(
        functools.partial(_out_route_kernel, n_lat=n_lat, route_ctx=route_ctx),
        out_shape=[sds((n_batch, s, d), F32), sds((n_batch * tiles, SORT_ROWS, HALF_D), jnp.int32),
                   sds((n_batch, s, LANES), F32), sds((n_batch * tiles, 1, LANES), jnp.int32)],
        grid=(n_batch, tiles),
        in_specs=[tok(d)] + [tok(W_MIX)] * 4 + _mod_specs(n_batch) + [
            full((1, d)), full((4 * W_MIX, d)), full((d, LANES)), full((1, LANES)),
            full((TOKEN_TILE, TOKEN_TILE)), full((LANES, LANES))],
        out_specs=[tok(d), per_tile(SORT_ROWS, HALF_D), tok(LANES), per_tile(1, LANES)],
        input_output_aliases={0: 0},
        compiler_params=_cparams(("parallel", "arbitrary")),
        name="out_route",
    )(x, *ys, mod, mod, g.reshape(1, d), w_out, w_route, b_route, tri, upper)


def _split3_bf16(a):
    p1 = a.astype(BF16)
    r1 = a - p1.astype(F32)
    p2 = r1.astype(BF16)
    return p1, p2, (r1 - p2.astype(F32)).astype(BF16)


def _moe_meta_kernel(cnt_ref, tril_ref, upper_ref, src_ref, dst_ref, blk_ref, used_ref, *, n_blocks):
    n_tiles = cnt_ref.shape[0]
    cnt = cnt_ref[...].astype(F32)
    upper = upper_ref[...]
    tiles_before = _dot2_rhs(tril_ref[...], cnt)
    total = jnp.sum(cnt, 0, keepdims=True)
    padded = jnp.floor((total + (EXPERT_TILE - 1)) * (1.0 / EXPERT_TILE)) * EXPERT_TILE
    parts = _split3_bf16(jnp.broadcast_to(padded, (8, LANES)))
    start = (_dot(parts[0], upper) + _dot(parts[1], upper) + _dot(parts[2], upper))[0:1]
    tile_row = lax.broadcasted_iota(jnp.int32, (n_tiles, 1), 0).astype(F32)
    src_ref[...] = (tile_row * SORT_ROWS + _dot2_lhs(cnt, upper)).astype(jnp.int32)
    dst_ref[...] = (start + tiles_before).astype(jnp.int32)
    lane = lax.broadcasted_iota(jnp.int32, (1, LANES), 1)
    is_expert = (lane >= N_GROUPS) & (lane < N_GROUPS + N_EXPERTS)
    end = start + padded
    blk0 = lax.broadcasted_iota(jnp.int32, (n_blocks, 1), 0).astype(F32) * EXPERT_TILE
    done = jnp.sum(jnp.where(is_expert & (end <= blk0), 1.0, 0.0), -1, keepdims=True)
    blk_ref[...] = jnp.minimum(done, N_EXPERTS - 1.0).astype(jnp.int32)
    used_ref[...] = (jnp.sum(padded, -1, keepdims=True) * (1.0 / EXPERT_TILE)).astype(jnp.int32)


def _moe_meta(counts, tril_tiles, upper, n_blocks):
    n_tiles = counts.shape[0]
    sds = jax.ShapeDtypeStruct
    src, dst, blk, used = pl.pallas_call(
        functools.partial(_moe_meta_kernel, n_blocks=n_blocks),
        out_shape=[sds((n_tiles, LANES), jnp.int32), sds((n_tiles, LANES), jnp.int32),
                   sds((n_blocks, 1), jnp.int32), sds((1, 1), jnp.int32)],
        name="moe_meta",
    )(counts.reshape(n_tiles, LANES), tril_tiles, upper)
    flat = lambda tab: tab[:, N_GROUPS:N_GROUPS + N_EXPERTS].reshape(n_tiles * N_EXPERTS)
    return (flat(counts.reshape(n_tiles, LANES)), flat(src), flat(dst)), blk.reshape(n_blocks), used.reshape(1)


RUN_BITS = SORT_ROWS.bit_length()


def _run_copies(n, src, src_row, dst, dst_row, sem, fn):
    for b in range(RUN_BITS - 1, RUN_ALIGN.bit_length() - 2, -1):
        size = 1 << b
        done = lax.shift_left(lax.shift_right_logical(n, b + 1), b + 1)

        @pl.when((n & size) != 0)
        def _():
            fn(pltpu.make_async_copy(src.at[pl.ds(pl.multiple_of(src_row + done, RUN_ALIGN), size), :],
                                     dst.at[pl.ds(pl.multiple_of(dst_row + done, RUN_ALIGN), size), :], sem))


def _tile_runs(cnt_ref, src_ref, dst_ref, tile, sorted_ref, slots_ref, sem, fn, to_slots, sorted_base=0):
    for e in range(N_EXPERTS):
        k = tile * N_EXPERTS + e
        in_sorted = src_ref[k] - sorted_base
        if to_slots:
            _run_copies(cnt_ref[k], sorted_ref, in_sorted, slots_ref, dst_ref[k], sem, fn)
        else:
            _run_copies(cnt_ref[k], slots_ref, dst_ref[k], sorted_ref, in_sorted, sem, fn)


def _to_slots_kernel(cnt_ref, src_ref, dst_ref, hs_ref, zero_ref, xs_ref, sem):
    del zero_ref
    t = pl.program_id(0)
    runs = functools.partial(_tile_runs, cnt_ref, src_ref, dst_ref, tile=t, sorted_ref=hs_ref.at[0],
                             slots_ref=xs_ref, sem=sem, to_slots=True, sorted_base=t * SORT_ROWS)
    runs(fn=lambda cp: cp.start())
    runs(fn=lambda cp: cp.wait())


def _to_slots(tables, hs_sorted, n_slots):
    n_tiles = hs_sorted.shape[0]
    return pl.pallas_call(
        _to_slots_kernel,
        out_shape=jax.ShapeDtypeStruct((n_slots, HALF_D), jnp.int32),
        grid_spec=pltpu.PrefetchScalarGridSpec(
            num_scalar_prefetch=3,
            grid=(n_tiles,),
            in_specs=[pl.BlockSpec((1, SORT_ROWS, HALF_D), lambda t, *_: (t, 0, 0)),
                      pl.BlockSpec(memory_space=pl.ANY)],
            out_specs=pl.BlockSpec(memory_space=pl.ANY),
            scratch_shapes=[pltpu.SemaphoreType.DMA(())]),
        input_output_aliases={4: 0},
        compiler_params=_cparams(("arbitrary",)),
        name="moe_to_slots",
    )(*tables, hs_sorted, jnp.zeros((n_slots, HALF_D), jnp.int32))


def _expert_kernel(be_ref, nb_ref, x_ref, wg_ref, wu_ref, wd_ref, y_ref, wgu_s, wd_s):
    i = pl.program_id(0)

    @pl.when((i == 0) | (be_ref[i] != be_ref[jnp.maximum(i - 1, 0)]))
    def _():
        wgu_s[:, 0:D_EXPERT] = wg_ref[0, 0].astype(BF16)
        wgu_s[:, D_EXPERT:] = wu_ref[0, 0].astype(BF16)
        wd_s[...] = wd_ref[0, 0].astype(BF16)

    @pl.when(i < nb_ref[0])
    def _():
        lo, hi = _unpack_pairs(x_ref[...])
        gu = _dot(lo, wgu_s[0:HALF_D, :]) + _dot(hi, wgu_s[HALF_D:, :])
        hid = _silu(gu[:, 0:D_EXPERT]) * gu[:, D_EXPERT:]
        y_ref[...] = _pack_pairs(_dot(hid.astype(BF16), wd_s[...]))

    @pl.when(i >= nb_ref[0])
    def _():
        y_ref[...] = jnp.zeros_like(y_ref)


def _experts(blk_expert, n_used, xs, w_gate, w_up, w_down, layer):
    n_slots = xs.shape[0]
    weight = lambda rows, cols: pl.BlockSpec((1, 1, rows, cols), lambda i, be, nb: (layer, be[i], 0, 0))
    return pl.pallas_call(
        _expert_kernel,
        out_shape=jax.ShapeDtypeStruct((n_slots, HALF_D), jnp.int32),
        grid_spec=pltpu.PrefetchScalarGridSpec(
            num_scalar_prefetch=2,
            grid=(n_slots // EXPERT_TILE,),
            in_specs=[pl.BlockSpec((EXPERT_TILE, HALF_D), lambda i, be, nb: (i, 0)),
                      weight(D_MODEL, D_EXPERT), weight(D_MODEL, D_EXPERT), weight(D_EXPERT, D_MODEL)],
            out_specs=pl.BlockSpec((EXPERT_TILE, HALF_D), lambda i, be, nb: (i, 0)),
            scratch_shapes=[pltpu.VMEM((D_MODEL, 2 * D_EXPERT), BF16), pltpu.VMEM((D_EXPERT, D_MODEL), BF16)]),
        compiler_params=_cparams(("arbitrary",)),
        name="moe_experts",
    )(blk_expert, n_used, xs, w_gate, w_up, w_down)


def _combine_kernel(cnt_ref, src_ref, dst_ref, x_ref, route_ref, mb_ref, mc_ref, ys_ref, xo_ref, buf, sem,
                    *, n_lat):
    tiles = pl.num_programs(1)
    t = pl.program_id(0) * tiles + pl.program_id(1)
    buf[...] = jnp.zeros_like(buf)
    runs = functools.partial(_tile_runs, cnt_ref, src_ref, dst_ref, tile=t, sorted_ref=buf, slots_ref=ys_ref,
                             sem=sem, to_slots=False, sorted_base=t * SORT_ROWS)
    runs(fn=lambda cp: cp.start())
    route = route_ref[0]
    col = lax.broadcasted_iota(jnp.int32, (1, SORT_ROWS), 1).astype(F32)
    weights = (jnp.where(col == route[:, ROUTE_P0:ROUTE_P0 + 1], route[:, ROUTE_G0:ROUTE_G0 + 1], 0.0)
               + jnp.where(col == route[:, ROUTE_P1:ROUTE_P1 + 1], route[:, ROUTE_G1:ROUTE_G1 + 1], 0.0)
               ).astype(BF16)
    runs(fn=lambda cp: cp.wait())
    lo, hi = _unpack_pairs(buf[...])
    moe = jnp.concatenate([_dot(weights, lo), _dot(weights, hi)], 1)
    is_ctx = _is_ctx_rows(pl.program_id(1), TOKEN_TILE, n_lat)
    xo_ref[0] = x_ref[0] + _mod_vec(mb_ref, mc_ref, 5, is_ctx) * moe


def _combine(x, route, mod, tables, ys, n_lat):
    n_batch, s, d = x.shape
    tiles = s // TOKEN_TILE
    tok = lambda width: pl.BlockSpec((1, TOKEN_TILE, width), lambda b, j, *_: (b, j, 0))
    mod_specs = [pl.BlockSpec((1, 1, N_MOD * d), lambda b, j, *_: (b, 0, 0)),
                 pl.BlockSpec((1, 1, N_MOD * d), lambda b, j, *_: (n_batch, 0, 0))]
    return pl.pallas_call(
        functools.partial(_combine_kernel, n_lat=n_lat),
        out_shape=jax.ShapeDtypeStruct((n_batch, s, d), F32),
        grid_spec=pltpu.PrefetchScalarGridSpec(
            num_scalar_prefetch=3,
            grid=(n_batch, tiles),
            in_specs=[tok(d), tok(LANES)] + mod_specs + [pl.BlockSpec(memory_space=pl.ANY)],
            out_specs=tok(d),
            scratch_shapes=[pltpu.VMEM((SORT_ROWS, HALF_D), jnp.int32), pltpu.SemaphoreType.DMA(())]),
        input_output_aliases={3: 0},
        compiler_params=_cparams(("arbitrary", "arbitrary")),
        name="moe_combine",
    )(*tables, x, route, mod, mod, ys)


def _final_kernel(x_ref, g_ref, o_ref):
    x = x_ref[0]
    o_ref[0] = x * lax.rsqrt(jnp.mean(x * x, -1, keepdims=True) + EPS) * g_ref[...]


def _final_norm(x, g, n_lat):
    n_batch, _, d = x.shape
    tm = 512
    return pl.pallas_call(
        _final_kernel,
        out_shape=jax.ShapeDtypeStruct((n_batch, n_lat, d), F32),
        grid=(n_batch, n_lat // tm),
        in_specs=[pl.BlockSpec((1, tm, d), lambda b, j: (b, j, 0)), pl.BlockSpec((1, d), lambda b, j: (0, 0))],
        out_specs=pl.BlockSpec((1, tm, d), lambda b, j: (b, j, 0)),
        compiler_params=_cparams(("parallel", "arbitrary")),
        name="final_norm",
    )(x, g.reshape(1, d))


def _block_diag(w):
    heads, di, dj = w.shape
    eye = jnp.eye(heads, dtype=w.dtype)
    return (eye[:, None, :, None] * w[:, :, None, :]).reshape(heads * di, heads * dj)


def _rope_tables(n_lat):
    rows = n_lat // GRID_W
    row = np.repeat(np.arange(rows, dtype=np.float64), GRID_W)
    col = np.tile(np.arange(GRID_W, dtype=np.float64), rows)
    inv = ROPE_BASE ** (-np.arange(ROPE_FREQS, dtype=np.float64) / ROPE_FREQS)
    ang = np.concatenate([row[:, None] * inv, col[:, None] * inv], -1)
    cos = np.tile(np.concatenate([np.cos(ang), np.cos(ang)], -1), (1, W_MIX // DIFF_DK))
    sin = np.tile(np.concatenate([-np.sin(ang), np.sin(ang)], -1), (1, W_MIX // DIFF_DK))
    return jnp.asarray(cos, F32), jnp.asarray(sin, F32)


def kernel(x, c, ctx, c_ctx, w_ada, b_ada, g_mix, g_ffn, w_in, w_out, a_conv_w, a_conv_b, a_w_r, a_b_r,
           a_w_i, a_b_i, a_lam, b_lq1, b_lk1, b_lq2, b_lk2, b_sub_g, c_conv_w, c_conv_b, c_ln_g, c_ln_b,
           c_w_pw, c_b_pw, d_conv_w, d_conv_b, d_w_f1, d_b_f1, d_freq, d_w_f2, d_b_f2, d_w_f3, d_decay,
           d_bias, moe_w_rg, moe_b_rg, moe_w_re, moe_b_re, moe_w_gate, moe_w_up, moe_w_down, g_final):
    n_batch, n_lat, d = x.shape
    n_ctx = ctx.shape[1]
    depth = w_ada.shape[0]
    s = n_lat + n_ctx
    assert d == D_MODEL and s % TOKEN_TILE == 0 and n_lat % CONV_BLOCK == 0 and n_ctx <= CONV_BLOCK
    assert n_lat % Q_TILE == 0 and n_ctx % LANES == 0 and n_lat % GRID_W == 0

    xs = jnp.concatenate([x, ctx], 1)
    mod_rows = -(-(n_batch + 1) // MOD_ROWS_PAD) * MOD_ROWS_PAD
    c_all = jnp.concatenate([c, c_ctx[None], jnp.zeros((mod_rows - n_batch - 1, d), F32)], 0)
    mod_all = _ada_table(c_all, w_ada, b_ada)

    rope_cos, rope_sin = _rope_tables(n_lat)
    fwd_np, inv_np, filt_np = _dft_matrices()
    fwd_m, inv_m, filt_m = (jnp.asarray(m, F32).astype(BF16) for m in (fwd_np, inv_np, filt_np))
    feats_l, aux_l = (jnp.asarray(m) for m in _hyena_positions(2 * n_lat, n_lat, n_lat))
    feats_c, aux_c = (jnp.asarray(m) for m in _hyena_positions(CONV_FFT, CONV_BLOCK, n_ctx))
    gmat = jnp.asarray(np.kron(np.eye(DIFF_HEADS), np.full((DIFF_DV, DIFF_DV), 1.0 / DIFF_DV)), F32).astype(BF16)
    tri = jnp.asarray(np.tril(np.ones((TOKEN_TILE, TOKEN_TILE)), -1), F32).astype(BF16)
    n_tiles = n_batch * (s // TOKEN_TILE)
    tril_tiles = jnp.asarray(np.tril(np.ones((n_tiles, n_tiles)), -1), F32).astype(BF16)
    upper = jnp.asarray(np.triu(np.ones((LANES, LANES)), 1), F32).astype(BF16)

    w1 = jnp.pad(d_w_f1, ((0, 0), (0, LANES - HYENA_EMB), (0, 0)))
    w3 = d_w_f3.reshape(depth, HYENA_FFN, HYENA_ORDER, 2 * W_MIX).transpose(0, 2, 1, 3)
    dec = d_decay.reshape(depth, HYENA_ORDER, 1, 2 * W_MIX)
    row = lambda v: v.reshape(depth, 1, -1)
    hy = _hyena_filters(feats_l, aux_l, feats_c, aux_c, w1, row(d_b_f1), row(d_freq), d_w_f2, row(d_b_f2),
                        w3, dec, filt_m)

    n_tok = n_batch * s
    for l in range(depth):
        last = l == depth - 1
        lam_init = 0.8 - 0.6 * math.exp(-0.3 * l)
        mod = mod_all[l].reshape(mod_rows, 1, N_MOD * d)
        p_a, p_b, p_c, p_d = _in_proj(xs, mod, g_mix[l], w_in[l].astype(BF16), n_lat)

        w_gates = jnp.stack([jnp.concatenate([_block_diag(a_w_r[l, dr]), _block_diag(a_w_i[l, dr])], 1)
                             for dr in range(2)]).astype(BF16)
        b_gates = jnp.concatenate([a_b_r[l], a_b_i[l]], -1)[:, None, :]
        y_a = _rglru(p_a, a_conv_w[l], a_conv_b[l][:, None, :], w_gates, b_gates, a_lam[l][:, None, :], n_lat)
        lqk = jnp.stack([b_lq1[l], b_lk1[l], b_lq2[l], b_lk2[l]])
        y_b = _diff_attn(p_b, rope_cos, rope_sin, lqk, jnp.tile(b_sub_g[l], DIFF_HEADS)[None], gmat,
                         n_lat, lam_init)
        y_c = _conformer(p_c, c_conv_w[l], c_conv_b[l][None], c_ln_g[l][None], c_ln_b[l][None],
                         c_w_pw[l].astype(BF16), c_b_pw[l][None], n_lat)
        y_d = _hyena(p_d, d_conv_w[l], d_conv_b[l][None], d_bias[l][:, None, :], [f[l] for f in hy],
                     fwd_m, inv_m, n_lat)

        w_route = jnp.pad(jnp.concatenate([moe_w_rg[l], moe_w_re[l]], 1),
                          ((0, 0), (0, LANES - N_GROUPS - N_EXPERTS)))
        b_route = jnp.pad(jnp.concatenate([moe_b_rg[l], moe_b_re[l]]), (0, LANES - N_GROUPS - N_EXPERTS))[None]
        xs, hs_sorted, route, counts = _out_route(xs, (y_a, y_b, y_c, y_d), mod, g_ffn[l], w_out[l].astype(BF16),
                                                  w_route, b_route, tri, upper, n_lat, not last)

        n_assign = 2 * (n_tok if not last else n_batch * n_lat)
        n_rows_max = n_assign + n_tiles * N_EXPERTS * (RUN_ALIGN - 1)
        n_slots = -(-n_rows_max // EXPERT_TILE) * EXPERT_TILE + N_EXPERTS * EXPERT_TILE
        tables, blk_expert, n_used = _moe_meta(counts, tril_tiles, upper, n_slots // EXPERT_TILE)
        slots = _to_slots(tables, hs_sorted, n_slots)
        ys = _experts(blk_expert, n_used, slots, moe_w_gate, moe_w_up, moe_w_down, l)
        xs = _combine(xs, route, mod, tables, ys, n_lat)

    return _final_norm(xs, g_final, n_lat)
```

```python
import functools
import math

import numpy as np
import jax
import jax.numpy as jnp
from jax import lax
from jax.experimental import pallas as pl
from jax.experimental.pallas import tpu as pltpu

F32 = jnp.float32
BF16 = jnp.bfloat16

D_MODEL = 1024
N_MOD = 6
EPS = 1e-6
W_MIX = 256
LRU_HEADS = 4
LRU_CONV = 4
RG_C = 8.0
DIFF_HEADS = 4
DIFF_DV = W_MIX // DIFF_HEADS
DIFF_DK = DIFF_DV // 2
ROPE_BASE = 10000.0
ROPE_FREQS = DIFF_DK // 4
GRID_W = 64
CONF_K = 31
HYENA_ORDER = 2
HYENA_SHORT = 3
HYENA_EMB = 33
HYENA_BANDS = (HYENA_EMB - 1) // 2
HYENA_FFN = 64
N_GROUPS = 4
EXPERTS_PER_GROUP = 8
N_EXPERTS = N_GROUPS * EXPERTS_PER_GROUP
D_EXPERT = 512
IN_A, IN_B, IN_C, IN_D = 2 * W_MIX, 3 * W_MIX, 2 * W_MIX, 3 * W_MIX
D_IN = IN_A + IN_B + IN_C + IN_D

LANES = 128
TOKEN_TILE = 768
Q_TILE = 512
KEY_CHUNK = 768
CONV_BLOCK = 512
CONV_FFT = 2 * CONV_BLOCK
EXPERT_TILE = 512
MOD_ROWS_PAD = 8
VMEM_LIMIT = 56 * 1024 * 1024


def _cparams(sem, vmem=VMEM_LIMIT):
    return pltpu.CompilerParams(dimension_semantics=sem, vmem_limit_bytes=vmem)


def _dot(a, b):
    return jnp.dot(a, b, preferred_element_type=F32)


def _split_bf16(a):
    hi = a.astype(BF16)
    lo = (a - hi.astype(F32)).astype(BF16)
    return hi, lo


def _dot3(a, b):
    ah, al = _split_bf16(a)
    bh, bl = _split_bf16(b)
    return _dot(ah, bh) + _dot(al, bh) + _dot(ah, bl)


def _dot2_lhs(a, b_bf16):
    ah, al = _split_bf16(a)
    return _dot(ah, b_bf16) + _dot(al, b_bf16)


def _dot2_rhs(a_bf16, b):
    bh, bl = _split_bf16(b)
    return _dot(a_bf16, bh) + _dot(a_bf16, bl)


def _sigmoid(x):
    return 1.0 / (1.0 + jnp.exp(-x))


def _silu(x):
    return x * _sigmoid(x)


def _gelu_tanh(x):
    return 0.5 * x * (1.0 + jnp.tanh(math.sqrt(2.0 / math.pi) * (x + 0.044715 * (x * x * x))))


def _is_ctx_rows(tile_idx, tile_rows, n_lat):
    rows = tile_idx * tile_rows + lax.broadcasted_iota(jnp.int32, (tile_rows, 1), 0)
    return rows >= n_lat


def _mod_vec(mb_ref, mc_ref, k, is_ctx):
    vb = mb_ref[0, :, k * D_MODEL:(k + 1) * D_MODEL]
    vc = mc_ref[0, :, k * D_MODEL:(k + 1) * D_MODEL]
    return jnp.where(is_ctx, vc, vb)


def _rms_modulate(x, g, shift, scale):
    y = x * lax.rsqrt(jnp.mean(x * x, -1, keepdims=True) + EPS)
    return (y * g) * (1.0 + scale) + shift


def _ada_kernel(c_ref, w_ref, b_ref, o_ref):
    o_ref[0] = _dot3(_silu(c_ref[...]), w_ref[0]) + b_ref[0]


def _ada_table(c_all, w_ada, b_ada):
    depth, d, n = w_ada.shape
    rows = c_all.shape[0]
    tn = 1536
    return pl.pallas_call(
        _ada_kernel,
        out_shape=jax.ShapeDtypeStruct((depth, rows, n), F32),
        grid=(depth, n // tn),
        in_specs=[pl.BlockSpec((rows, d), lambda l, j: (0, 0)),
                  pl.BlockSpec((1, d, tn), lambda l, j: (l, 0, j)),
                  pl.BlockSpec((1, 1, tn), lambda l, j: (l, 0, j))],
        out_specs=pl.BlockSpec((1, rows, tn), lambda l, j: (l, 0, j)),
        compiler_params=_cparams(("arbitrary", "arbitrary")),
        name="ada_table",
    )(c_all, w_ada, b_ada.reshape(depth, 1, n))


def _in_proj_kernel(x_ref, mb_ref, mc_ref, g_ref, w_ref, pa_ref, pb_ref, pc_ref, pd_ref, *, n_lat):
    is_ctx = _is_ctx_rows(pl.program_id(1), TOKEN_TILE, n_lat)
    h = _rms_modulate(x_ref[0], g_ref[...], _mod_vec(mb_ref, mc_ref, 0, is_ctx),
                      _mod_vec(mb_ref, mc_ref, 1, is_ctx)).astype(BF16)
    col = 0
    for ref, width in ((pa_ref, IN_A), (pb_ref, IN_B), (pc_ref, IN_C), (pd_ref, IN_D)):
        ref[0] = _dot(h, w_ref[:, col:col + width]).astype(BF16)
        col += width


def _mod_specs(n_batch):
    width = N_MOD * D_MODEL
    return [pl.BlockSpec((1, 1, width), lambda b, j: (b, 0, 0)),
            pl.BlockSpec((1, 1, width), lambda b, j: (n_batch, 0, 0))]


def _in_proj(x, mod, g, w, n_lat):
    n_batch, s, d = x.shape
    widths = (IN_A, IN_B, IN_C, IN_D)
    return pl.pallas_call(
        functools.partial(_in_proj_kernel, n_lat=n_lat),
        out_shape=[jax.ShapeDtypeStruct((n_batch, s, wd), BF16) for wd in widths],
        grid=(n_batch, s // TOKEN_TILE),
        in_specs=[pl.BlockSpec((1, TOKEN_TILE, d), lambda b, j: (b, j, 0))] + _mod_specs(n_batch) + [
            pl.BlockSpec((1, d), lambda b, j: (0, 0)),
            pl.BlockSpec((d, D_IN), lambda b, j: (0, 0))],
        out_specs=[pl.BlockSpec((1, TOKEN_TILE, wd), lambda b, j: (b, j, 0)) for wd in widths],
        compiler_params=_cparams(("parallel", "arbitrary")),
        name="in_proj",
    )(x, mod, mod, g.reshape(1, d), w)


SUBLANES = 8


def _group_scan(a, b, reverse):
    n, w = a.shape
    a = a.reshape(n // SUBLANES, SUBLANES, w)
    b = b.reshape(n // SUBLANES, SUBLANES, w)
    row = lax.broadcasted_iota(jnp.int32, (1, SUBLANES, 1), 1)
    d = 1
    while d < SUBLANES:
        keep = (row < SUBLANES - d) if reverse else (row >= d)
        shift = SUBLANES - d if reverse else d
        b = a * jnp.where(keep, pltpu.roll(b, shift, 1), 0.0) + b
        a = a * jnp.where(keep, pltpu.roll(a, shift, 1), 1.0)
        d *= 2
    return a.reshape(n, w), b.reshape(n, w)


def _chain_groups(a_ref, b_ref, h_ref, row0, n, h0, reverse):
    groups = n // SUBLANES

    def body(g, carry):
        gg = (groups - 1 - g) if reverse else g
        rows = pl.ds(pl.multiple_of(row0 + gg * SUBLANES, SUBLANES), SUBLANES)
        h = b_ref[rows, :] + a_ref[rows, :] * carry
        h_ref[rows, :] = h
        return h[0:1] if reverse else h[SUBLANES - 1:SUBLANES]

    return lax.fori_loop(0, groups, body, h0)


def _rglru_kernel(p_ref, cw_ref, cb_ref, wg_ref, bg_ref, lam_ref, o_ref,
                  xpad, a_s, b_s, hf_s, hb_s, *, n_lat, n_ctx):
    pad = 8
    lat0, ctx0 = pad, 2 * pad + n_lat
    zeros = jnp.zeros((pad, W_MIX), F32)
    xpad[0:pad, :] = zeros
    xpad[lat0:lat0 + n_lat, :] = p_ref[0, 0:n_lat, 0:W_MIX].astype(F32)
    xpad[lat0 + n_lat:ctx0, :] = zeros
    xpad[ctx0:ctx0 + n_ctx, :] = p_ref[0, n_lat:n_lat + n_ctx, 0:W_MIX].astype(F32)
    xpad[ctx0 + n_ctx:ctx0 + n_ctx + pad, :] = zeros

    for d, reverse in enumerate((False, True)):
        def conv(off, n):
            acc = cb_ref[d]
            for k in range(LRU_CONV):
                sh = k if reverse else k - (LRU_CONV - 1)
                acc = acc + cw_ref[d, k:k + 1, :] * xpad[off + sh:off + sh + n, :]
            return acc

        u = jnp.concatenate([conv(lat0, n_lat), conv(ctx0, n_ctx)], 0)
        gates = _dot(u.astype(BF16), wg_ref[d]) + bg_ref[d]
        r = _sigmoid(gates[:, 0:W_MIX])
        i = _sigmoid(gates[:, W_MIX:2 * W_MIX])
        z = -lam_ref[d]
        softplus = jnp.maximum(z, 0.0) + jnp.log(1.0 + jnp.exp(-jnp.abs(z)))
        a = jnp.exp(-RG_C * r * softplus)
        b = jnp.sqrt(1.0 - a * a) * (i * u)
        a_s[...], b_s[...] = _group_scan(a, b, reverse)
        h_out = hb_s if reverse else hf_s
        h_ctx = _chain_groups(a_s, b_s, h_out, n_lat, n_ctx, jnp.zeros((1, W_MIX), F32), reverse)
        _chain_groups(a_s, b_s, h_out, 0, n_lat, h_ctx, reverse)

    xg = p_ref[0, :, W_MIX:2 * W_MIX].astype(F32)
    o_ref[0] = ((hf_s[...] + hb_s[...]) * _gelu_tanh(xg)).astype(BF16)


def _rglru(p_a, conv_w, conv_b, w_gates, b_gates, lam, n_lat):
    n_batch, s, _ = p_a.shape
    n_ctx = s - n_lat
    full = lambda shape: pl.BlockSpec(shape, lambda b: (0,) * len(shape))
    return pl.pallas_call(
        functools.partial(_rglru_kernel, n_lat=n_lat, n_ctx=n_ctx),
        out_shape=jax.ShapeDtypeStruct((n_batch, s, W_MIX), BF16),
        grid=(n_batch,),
        in_specs=[pl.BlockSpec((1, s, IN_A), lambda b: (b, 0, 0)),
                  full((2, LRU_CONV, W_MIX)), full((2, 1, W_MIX)),
                  full((2, W_MIX, 2 * W_MIX)), full((2, 1, 2 * W_MIX)), full((2, 1, W_MIX))],
        out_specs=pl.BlockSpec((1, s, W_MIX), lambda b: (b, 0, 0)),
        scratch_shapes=[pltpu.VMEM((s + 24, W_MIX), F32)] + [pltpu.VMEM((s, W_MIX), F32)] * 4,
        compiler_params=_cparams(("parallel",)),
        name="rglru",
    )(p_a, conv_w, conv_b, w_gates, b_gates, lam)


def _diff_attn_kernel(p_ref, cos_ref, sin_ref, lqk_ref, subg_ref, gmat_ref, o_ref, q_s, k_s, s_a, s_b, e_buf,
                      *, n_lat, n_ctx, lam_init):
    lane = lax.broadcasted_iota(jnp.int32, (1, W_MIX), 1)
    first_half = (lane % DIFF_DK) < (DIFF_DK // 2)

    def rope(x):
        partner = jnp.where(first_half, pltpu.roll(x, W_MIX - DIFF_DK // 2, 1),
                            pltpu.roll(x, DIFF_DK // 2, 1))
        return x * cos_ref[...] + partner * sin_ref[...]

    scale = DIFF_DK ** -0.5 * math.log2(math.e)
    q = p_ref[0, :, 0:W_MIX].astype(F32)
    k = p_ref[0, :, W_MIX:2 * W_MIX].astype(F32)
    q = (jnp.concatenate([rope(q[0:n_lat]), q[n_lat:]], 0) * scale).astype(BF16)
    for g in range(2 * DIFF_HEADS):
        q_s[g] = q[:, g * DIFF_DK:(g + 1) * DIFF_DK]
    k_s[...] = jnp.concatenate([rope(k[0:n_lat]), k[n_lat:]], 0).T.astype(BF16)

    lqk = lqk_ref[...]
    lam = (jnp.exp(jnp.sum(lqk[0:1] * lqk[1:2], -1, keepdims=True))
           - jnp.exp(jnp.sum(lqk[2:3] * lqk[3:4], -1, keepdims=True)) + lam_init)

    n_stages = 2 * DIFF_HEADS
    s_bufs = (s_a, s_b)

    def attend(q0, nq, kv0, nk):
        chunks = [(c0, min(KEY_CHUNK, nk - c0)) for c0 in range(0, nk, KEY_CHUNK)]

        def scores(g, c0, cw, top):
            s = _dot(q_s[g, pl.ds(q0, nq), :], k_s[g * DIFF_DK:(g + 1) * DIFF_DK, kv0 + c0:kv0 + c0 + cw])
            s_bufs[g % 2][0:nq, c0:c0 + cw] = s
            return s if top is None else jnp.maximum(top, s)

        def probs(g, c0, cw, m, tot):
            e = jnp.exp2(s_bufs[g % 2][0:nq, c0:c0 + cw] - m)
            e_buf[g % 4, 0:nq, c0:c0 + cw] = e.astype(BF16)
            return e if tot is None else tot + e

        def weighted(head, c0, cw, out):
            h, r0, r1 = head
            b = 2 * (h % 2)
            w = e_buf[b, 0:nq, c0:c0 + cw] * r0 - e_buf[b + 1, 0:nq, c0:c0 + cw] * r1
            part = _dot(w, p_ref[0, kv0 + c0:kv0 + c0 + cw, 2 * W_MIX:3 * W_MIX])
            return part if out is None else out + part

        top = None
        for c0, cw in chunks:
            top = scores(0, c0, cw, top)
        m = jnp.max(top, -1, keepdims=True)
        acc = jnp.zeros((nq, W_MIX), F32)
        head, r_first = None, None
        for g in range(n_stages + 1):
            top, tot, out = None, None, None
            for c0, cw in chunks:
                if g + 1 < n_stages:
                    top = scores(g + 1, c0, cw, top)
                if g < n_stages:
                    tot = probs(g, c0, cw, m, tot)
                if head is not None:
                    out = weighted(head, c0, cw, out)
            if head is not None:
                acc = jnp.where(lane // DIFF_DV == head[0], out, acc)
                head = None
            if g < n_stages:
                r = 1.0 / jnp.sum(tot, -1, keepdims=True)
                if g % 2 == 0:
                    r_first = r
                else:
                    head = (g // 2, r_first.astype(BF16), (lam * r).astype(BF16))
            if g + 1 < n_stages:
                m = jnp.max(top, -1, keepdims=True)
        ms = _dot2_lhs(acc * acc, gmat_ref[...])
        y = acc * lax.rsqrt(ms + EPS) * subg_ref[...] * (1.0 - lam_init)
        o_ref[0, pl.ds(q0, nq), :] = y.astype(BF16)

    def lat_tile(i, carry):
        attend(pl.multiple_of(i * Q_TILE, Q_TILE), Q_TILE, 0, n_lat + n_ctx)
        return carry

    lax.fori_loop(0, n_lat // Q_TILE, lat_tile, 0)
    attend(n_lat, n_ctx, n_lat, n_ctx)


def _diff_attn(p_b, rope_cos, rope_sin, lqk, sub_g, gmat, n_lat, lam_init):
    n_batch, s, _ = p_b.shape
    n_ctx = s - n_lat
    full = lambda shape: pl.BlockSpec(shape, lambda b: (0,) * len(shape))
    return pl.pallas_call(
        functools.partial(_diff_attn_kernel, n_lat=n_lat, n_ctx=n_ctx, lam_init=lam_init),
        out_shape=jax.ShapeDtypeStruct((n_batch, s, W_MIX), BF16),
        grid=(n_batch,),
        in_specs=[pl.BlockSpec((1, s, IN_B), lambda b: (b, 0, 0)),
                  full((n_lat, W_MIX)), full((n_lat, W_MIX)), full((4, DIFF_DK)),
                  full((1, W_MIX)), full((W_MIX, W_MIX))],
        out_specs=pl.BlockSpec((1, s, W_MIX), lambda b: (b, 0, 0)),
        scratch_shapes=[pltpu.VMEM((2 * DIFF_HEADS, s, DIFF_DK), BF16), pltpu.VMEM((W_MIX, s), BF16),
                        pltpu.VMEM((Q_TILE, s), F32), pltpu.VMEM((Q_TILE, s), F32),
                        pltpu.VMEM((4, Q_TILE, s), BF16)],
        compiler_params=_cparams(("parallel",)),
        name="diff_attn",
    )(p_b, rope_cos, rope_sin, lqk, sub_g, gmat)


def _conformer_kernel(p_ref, cw_ref, cb_ref, lng_ref, lnb_ref, wpw_ref, bpw_ref, o_ref, upad,
                      *, n_lat, n_ctx):
    pad = 16
    half = CONF_K // 2
    lat0, ctx0 = pad, 2 * pad + n_lat
    val = p_ref[0, :, 0:W_MIX].astype(F32)
    gate = p_ref[0, :, W_MIX:2 * W_MIX].astype(F32)
    u = val * _sigmoid(gate)
    zeros = jnp.zeros((pad, W_MIX), F32)
    upad[0:pad, :] = zeros
    upad[lat0:lat0 + n_lat, :] = u[0:n_lat]
    upad[lat0 + n_lat:ctx0, :] = zeros
    upad[ctx0:ctx0 + n_ctx, :] = u[n_lat:]
    upad[ctx0 + n_ctx:ctx0 + n_ctx + pad, :] = zeros

    def conv(off, n):
        acc = cb_ref[...]
        for r in range(SUBLANES):
            group = None
            for k in range(CONF_K):
                if (k - half) % SUBLANES == r:
                    base = off + (k - half) - r
                    term = cw_ref[k:k + 1, :] * upad[base:base + n + SUBLANES, :]
                    group = term if group is None else group + term
            acc = acc + group[r:r + n]
        return acc

    y = jnp.concatenate([conv(lat0, n_lat), conv(ctx0, n_ctx)], 0)
    mu = jnp.mean(y, -1, keepdims=True)
    yc = y - mu
    var = jnp.mean(yc * yc, -1, keepdims=True)
    y = yc * lax.rsqrt(var + 1e-5) * lng_ref[...] + lnb_ref[...]
    o_ref[0] = (_dot(_silu(y).astype(BF16), wpw_ref[...]) + bpw_ref[...]).astype(BF16)


def _conformer(p_c, conv_w, conv_b, ln_g, ln_b, w_pw, b_pw, n_lat):
    n_batch, s, _ = p_c.shape
    full = lambda shape: pl.BlockSpec(shape, lambda b: (0,) * len(shape))
    return pl.pallas_call(
        functools.partial(_conformer_kernel, n_lat=n_lat, n_ctx=s - n_lat),
        out_shape=jax.ShapeDtypeStruct((n_batch, s, W_MIX), BF16),
        grid=(n_batch,),
        in_specs=[pl.BlockSpec((1, s, IN_C), lambda b: (b, 0, 0)),
                  full((CONF_K, W_MIX)), full((1, W_MIX)), full((1, W_MIX)), full((1, W_MIX)),
                  full((W_MIX, W_MIX)), full((1, W_MIX))],
        out_specs=pl.BlockSpec((1, s, W_MIX), lambda b: (b, 0, 0)),
        scratch_shapes=[pltpu.VMEM((s + 48, W_MIX), F32)],
        compiler_params=_cparams(("parallel",)),
        name="conformer",
    )(p_c, conv_w, conv_b, ln_g, ln_b, w_pw, b_pw)


def _dft_matrices():
    n, hb = CONV_FFT, CONV_BLOCK
    k = np.arange(hb, dtype=np.float64)[:, None]
    t = np.arange(n, dtype=np.float64)[None, :]
    ang = 2.0 * np.pi * k * t / n
    full = np.concatenate([np.cos(ang), -np.sin(ang)], 0)
    full[hb, :] = np.cos(np.pi * t[0])
    fwd = full[:, :hb]
    wk = np.where(k == 0, 1.0, 2.0) / n
    tt = np.arange(hb, dtype=np.float64)[None, :]
    ang_i = 2.0 * np.pi * k * tt / n
    inv = np.concatenate([wk * np.cos(ang_i), -wk * np.sin(ang_i)], 0)
    inv[hb, :] = np.cos(np.pi * tt[0]) / n
    sign = np.cos(np.pi * k)
    filt = np.concatenate([sign * np.cos(ang), -sign * np.sin(ang)], 0)
    filt[hb, :] = np.cos(np.pi * t[0])
    filt[:, 0] = 0.0
    return fwd.astype(np.float32), inv.T.copy().astype(np.float32), filt.astype(np.float32)


def _hyena_positions(n_rows, center, seq_len):
    lag = np.arange(n_rows, dtype=np.float64) - center
    t = np.abs(lag)
    valid = t < seq_len
    t_unit = t / max(seq_len - 1, 1)
    bands = np.linspace(1e-4, HYENA_BANDS - 1, HYENA_BANDS)
    ang = (2.0 * np.pi / seq_len) * t[:, None] * bands[None, :]
    feats = np.zeros((n_rows, LANES), np.float64)
    feats[:, 0] = t_unit
    feats[:, 1:1 + HYENA_BANDS] = np.cos(ang)
    feats[:, 1 + HYENA_BANDS:HYENA_EMB] = -np.sin(ang)
    aux = np.zeros((n_rows, LANES), np.float64)
    aux[:, 0] = t_unit
    aux[:, 1] = valid
    aux[:, 2] = lag >= 0
    return feats.astype(np.float32), aux.astype(np.float32)


def _hyena_filter_kernel(fl_ref, al_ref, fc_ref, ac_ref, w1_ref, b1_ref, fr_ref, w2_ref, b2_ref,
                         w3_ref, dec_ref, filt_ref, a_lat, b_lat, n_lat, a_ctx, b_ctx, n_ctx):
    filt = filt_ref[...]
    row0 = lax.broadcasted_iota(jnp.int32, (CONV_BLOCK, 1), 0) == 0

    def taps(feats_ref, aux_ref):
        f = jnp.sin(fr_ref[0] * (_dot3(feats_ref[...], w1_ref[0]) + b1_ref[0]))
        f = jnp.sin(fr_ref[0] * (_dot3(f, w2_ref[0]) + b2_ref[0]))
        t_unit = aux_ref[:, 0:1]
        h = _dot3(f, w3_ref[0, 0]) * jnp.exp(-t_unit * jnp.abs(dec_ref[0, 0]))
        g = jnp.where(aux_ref[:, 2:3] > 0.5, h[:, 0:W_MIX], h[:, W_MIX:2 * W_MIX]) * aux_ref[:, 1:2]
        energy = jnp.sum(g * g, 0, keepdims=True)
        return g * lax.rsqrt(energy + EPS)

    def spectrum(window, a_out, b_out, n_out):
        hp = _dot2_rhs(filt, window)
        a_out[...] = hp[0:CONV_BLOCK]
        b_out[...] = jnp.where(row0, 0.0, hp[CONV_BLOCK:])
        n_out[...] = hp[CONV_BLOCK:CONV_BLOCK + 1]

    g = taps(fl_ref, al_ref)
    for dd in range(a_lat.shape[2]):
        spectrum(g[dd * CONV_BLOCK:dd * CONV_BLOCK + CONV_FFT],
                 a_lat.at[0, 0, dd], b_lat.at[0, 0, dd], n_lat.at[0, 0, dd])
    spectrum(taps(fc_ref, ac_ref), a_ctx.at[0, 0], b_ctx.at[0, 0], n_ctx.at[0, 0])


def _hyena_filters(feats_l, aux_l, feats_c, aux_c, w1, b1, freq, w2, b2, w3, decay, filt_m):
    depth = w1.shape[0]
    rows_l = feats_l.shape[0]
    n_off = rows_l // CONV_BLOCK - 1
    full = lambda shape: pl.BlockSpec(shape, lambda l, o: (0,) * len(shape))
    per_layer = lambda shape: pl.BlockSpec((1,) + shape, lambda l, o: (l,) + (0,) * len(shape))
    spec5 = lambda r: pl.BlockSpec((1, 1, n_off, r, W_MIX), lambda l, o: (l, o, 0, 0, 0))
    spec4 = lambda r: pl.BlockSpec((1, 1, r, W_MIX), lambda l, o: (l, o, 0, 0))
    sds = jax.ShapeDtypeStruct
    return pl.pallas_call(
        _hyena_filter_kernel,
        out_shape=[sds((depth, HYENA_ORDER, n_off, CONV_BLOCK, W_MIX), F32),
                   sds((depth, HYENA_ORDER, n_off, CONV_BLOCK, W_MIX), F32),
                   sds((depth, HYENA_ORDER, n_off, 1, W_MIX), F32),
                   sds((depth, HYENA_ORDER, CONV_BLOCK, W_MIX), F32),
                   sds((depth, HYENA_ORDER, CONV_BLOCK, W_MIX), F32),
                   sds((depth, HYENA_ORDER, 1, W_MIX), F32)],
        grid=(depth, HYENA_ORDER),
        in_specs=[full(feats_l.shape), full(aux_l.shape), full(feats_c.shape), full(aux_c.shape),
                  per_layer((LANES, HYENA_FFN)), per_layer((1, HYENA_FFN)), per_layer((1, HYENA_FFN)),
                  per_layer((HYENA_FFN, HYENA_FFN)), per_layer((1, HYENA_FFN)),
                  pl.BlockSpec((1, 1, HYENA_FFN, 2 * W_MIX), lambda l, o: (l, o, 0, 0)),
                  pl.BlockSpec((1, 1, 1, 2 * W_MIX), lambda l, o: (l, o, 0, 0)),
                  full(filt_m.shape)],
        out_specs=[spec5(CONV_BLOCK), spec5(CONV_BLOCK), spec5(1),
                   spec4(CONV_BLOCK), spec4(CONV_BLOCK), spec4(1)],
        compiler_params=_cparams(("arbitrary", "arbitrary")),
        name="hyena_filters",
    )(feats_l, aux_l, feats_c, aux_c, w1, b1, freq, w2, b2, w3, decay, filt_m)


def _hyena_kernel(p_ref, cw_ref, cb_ref, bias_ref, al_ref, bl_ref, nl_ref, ac_ref, bc_ref, nc_ref,
                  fwd_ref, inv_ref, o_ref, upad, *, n_lat, n_ctx):
    pad = 8
    lat0, ctx0 = pad, 2 * pad + n_lat
    n_blk = n_lat // CONV_BLOCK
    zeros = jnp.zeros((pad, IN_D), F32)
    upad[0:pad, :] = zeros
    upad[lat0:lat0 + n_lat, :] = p_ref[0, 0:n_lat, :].astype(F32)
    upad[lat0 + n_lat:ctx0, :] = zeros
    upad[ctx0:ctx0 + n_ctx, :] = p_ref[0, n_lat:n_lat + n_ctx, :].astype(F32)
    upad[ctx0 + n_ctx:ctx0 + n_ctx + pad, :] = zeros

    def conv(off, n):
        acc = cb_ref[...]
        for k in range(HYENA_SHORT):
            sh = k - HYENA_SHORT // 2
            acc = acc + cw_ref[k:k + 1, :] * upad[off + sh:off + sh + n, :]
        return acc

    u = jnp.concatenate([conv(lat0, n_lat), conv(ctx0, n_ctx)], 0)
    row0 = lax.broadcasted_iota(jnp.int32, (CONV_BLOCK, 1), 0) == 0
    fwd = fwd_ref[...]
    inv = inv_ref[...]

    def block_spectrum(zb):
        zf = _dot(fwd, zb.astype(BF16))
        return zf[0:CONV_BLOCK], zf[CONV_BLOCK:]

    def block_output(pre, pim, pnyq):
        pim = jnp.where(row0, pnyq, pim)
        return _dot(inv, jnp.concatenate([pre, pim], 0).astype(BF16))

    z = u[:, 0:W_MIX]
    for o in range(HYENA_ORDER):
        gate = u[:, (o + 1) * W_MIX:(o + 2) * W_MIX]
        spec = [block_spectrum(z[jb * CONV_BLOCK:(jb + 1) * CONV_BLOCK]) for jb in range(n_blk)]
        outs = []
        for ib in range(n_blk):
            pre = pim = pnyq = None
            for jb in range(n_blk):
                dd = ib - jb + n_blk - 1
                a, b, nq = al_ref[o, dd], bl_ref[o, dd], nl_ref[o, dd]
                zre, zim = spec[jb]
                t_re = zre * a - zim * b
                t_im = zre * b + zim * a
                t_ny = zim[0:1] * nq
                pre = t_re if pre is None else pre + t_re
                pim = t_im if pim is None else pim + t_im
                pnyq = t_ny if pnyq is None else pnyq + t_ny
            outs.append(block_output(pre, pim, pnyq))
        zc = jnp.concatenate([z[n_lat:], jnp.zeros((CONV_BLOCK - n_ctx, W_MIX), F32)], 0)
        zre, zim = block_spectrum(zc)
        a, b, nq = ac_ref[o], bc_ref[o], nc_ref[o]
        outs.append(block_output(zre * a - zim * b, zre * b + zim * a, zim[0:1] * nq)[0:n_ctx])
        y = jnp.concatenate(outs, 0)
        z = gate * (y + bias_ref[o] * z)
    o_ref[0] = z.astype(BF16)


def _hyena(p_d, conv_w, conv_b, bias, filters, fwd_m, inv_m, n_lat):
    n_batch, s, _ = p_d.shape
    n_off = 2 * (n_lat // CONV_BLOCK) - 1
    full = lambda shape: pl.BlockSpec(shape, lambda b: (0,) * len(shape))
    a_lat, b_lat, n_lat_f, a_ctx, b_ctx, n_ctx_f = filters
    return pl.pallas_call(
        functools.partial(_hyena_kernel, n_lat=n_lat, n_ctx=s - n_lat),
        out_shape=jax.ShapeDtypeStruct((n_batch, s, W_MIX), BF16),
        grid=(n_batch,),
        in_specs=[pl.BlockSpec((1, s, IN_D), lambda b: (b, 0, 0)),
                  full((HYENA_SHORT, IN_D)), full((1, IN_D)), full((HYENA_ORDER, 1, W_MIX)),
                  full((HYENA_ORDER, n_off, CONV_BLOCK, W_MIX)), full((HYENA_ORDER, n_off, CONV_BLOCK, W_MIX)),
                  full((HYENA_ORDER, n_off, 1, W_MIX)),
                  full((HYENA_ORDER, CONV_BLOCK, W_MIX)), full((HYENA_ORDER, CONV_BLOCK, W_MIX)),
                  full((HYENA_ORDER, 1, W_MIX)),
                  full((CONV_FFT, CONV_BLOCK)), full((CONV_BLOCK, CONV_FFT))],
        out_specs=pl.BlockSpec((1, s, W_MIX), lambda b: (b, 0, 0)),
        scratch_shapes=[pltpu.VMEM((s + 24, IN_D), F32)],
        compiler_params=_cparams(("parallel",)),
        name="hyena",
    )(p_d, conv_w, conv_b, bias, a_lat, b_lat, n_lat_f, a_ctx, b_ctx, n_ctx_f, fwd_m, inv_m)


ROUTE_P0, ROUTE_P1, ROUTE_G0, ROUTE_G1 = range(4)
RUN_ALIGN = 8
SORT_ROWS = 2 * TOKEN_TILE + N_EXPERTS * RUN_ALIGN
HALF_D = D_MODEL // 2


def _pack_pairs(v):
    bits = lax.bitcast_convert_type(v.astype(BF16).astype(F32), jnp.int32)
    return lax.shift_right_logical(bits[:, 0:HALF_D], 16) | bits[:, HALF_D:]


def _unpack_pairs(w):
    lo = lax.bitcast_convert_type(lax.shift_left(w, 16), F32).astype(BF16)
    hi = lax.bitcast_convert_type(w & jnp.int32(-65536), F32).astype(BF16)
    return lo, hi


def _out_route_kernel(x_ref, ya_ref, yb_ref, yc_ref, yd_ref, mb_ref, mc_ref, g_ref, wo_ref, wr_ref,
                      br_ref, tri_ref, upper_ref, xo_ref, hs_ref, route_ref, cnt_ref, *, n_lat, route_ctx):
    is_ctx = _is_ctx_rows(pl.program_id(1), TOKEN_TILE, n_lat)
    mix = None
    for g, ref in enumerate((ya_ref, yb_ref, yc_ref, yd_ref)):
        part = _dot(ref[0], wo_ref[g * W_MIX:(g + 1) * W_MIX, :])
        mix = part if mix is None else mix + part
    x = x_ref[0] + _mod_vec(mb_ref, mc_ref, 2, is_ctx) * mix
    xo_ref[0] = x
    h = _rms_modulate(x, g_ref[...], _mod_vec(mb_ref, mc_ref, 3, is_ctx), _mod_vec(mb_ref, mc_ref, 4, is_ctx))

    logits = _dot3(h, wr_ref[...]) + br_ref[...]
    lane = lax.broadcasted_iota(jnp.int32, (1, LANES), 1)
    neg = jnp.float32(-jnp.inf)

    def top1(vals):
        m = jnp.max(vals, -1, keepdims=True)
        idx = jnp.min(jnp.where(vals == m, lane, LANES), -1, keepdims=True)
        return m, idx

    grp = jnp.where(lane < N_GROUPS, logits, neg)
    m_g, g_idx = top1(grp)
    p_grp = 1.0 / jnp.sum(jnp.exp(grp - m_g), -1, keepdims=True)
    lo = N_GROUPS + g_idx * EXPERTS_PER_GROUP
    exp_l = jnp.where((lane >= lo) & (lane < lo + EXPERTS_PER_GROUP), logits, neg)
    m1, i1 = top1(exp_l)
    m2, i2 = top1(jnp.where(lane == i1, neg, exp_l))
    e2 = jnp.exp(m2 - m1)
    gate0 = p_grp / (1.0 + e2)
    gate1 = p_grp * e2 / (1.0 + e2)

    live = jnp.ones_like(is_ctx) if route_ctx else jnp.logical_not(is_ctx)
    oh0 = jnp.where((lane == i1) & live, 1.0, 0.0)
    oh1 = jnp.where((lane == i2) & live, 1.0, 0.0)
    both = jnp.concatenate([oh0, oh1], 1).astype(BF16)
    before = _dot(tri_ref[...], both)
    cnt0 = jnp.sum(oh0, 0, keepdims=True)
    cnt = cnt0 + jnp.sum(oh1, 0, keepdims=True)
    run = jnp.floor((cnt + (RUN_ALIGN - 1)) * (1.0 / RUN_ALIGN)) * RUN_ALIGN
    cnt_ref[0] = run.astype(jnp.int32)
    lower = _dot2_lhs(jnp.broadcast_to(run, (8, LANES)), upper_ref[...])[0:1]
    pos0 = jnp.sum((before[:, 0:LANES] + lower) * oh0, -1, keepdims=True)
    pos1 = jnp.sum((before[:, LANES:] + cnt0 + lower) * oh1, -1, keepdims=True)
    pos0 = jnp.where(live, pos0, -1.0)
    pos1 = jnp.where(live, pos1, -1.0)

    slab = jnp.zeros((TOKEN_TILE, LANES), F32)
    for col, v in ((ROUTE_P0, pos0), (ROUTE_P1, pos1), (ROUTE_G0, gate0), (ROUTE_G1, gate1)):
        slab = jnp.where(lane == col, v, slab)
    route_ref[0] = slab

    slab_t = slab.T
    q = lax.broadcasted_iota(jnp.int32, (SORT_ROWS, 1), 0).astype(F32)
    sel = (q == slab_t[ROUTE_P0:ROUTE_P0 + 1]) | (q == slab_t[ROUTE_P1:ROUTE_P1 + 1])
    sel = jnp.where(sel, 1.0, 0.0).astype(BF16)
    hs_ref[0] = _pack_pairs(_dot(sel, h.astype(BF16)))


def _out_route(x, ys, mod, g, w_out, w_route, b_route, tri, upper, n_lat, route_ctx):
    n_batch, s, d = x.shape
    tiles = s // TOKEN_TILE
    tok = lambda width: pl.BlockSpec((1, TOKEN_TILE, width), lambda b, j: (b, j, 0))
    full = lambda shape: pl.BlockSpec(shape, lambda b, j: (0,) * len(shape))
    per_tile = lambda rows, width: pl.BlockSpec((1, rows, width), lambda b, j: (b * tiles + j, 0, 0))
    sds = jax.ShapeDtypeStruct
    return pl.pallas_call(
        functools.partial(_out_route_kernel, n_lat=n_lat, route_ctx=route_ctx),
        out_shape=[sds((n_batch, s, d), F32), sds((n_batch * tiles, SORT_ROWS, HALF_D), jnp.int32),
                   sds((n_batch, s, LANES), F32), sds((n_batch * tiles, 1, LANES), jnp.int32)],
        grid=(n_batch, tiles),
        in_specs=[tok(d)] + [tok(W_MIX)] * 4 + _mod_specs(n_batch) + [
            full((1, d)), full((4 * W_MIX, d)), full((d, LANES)), full((1, LANES)),
            full((TOKEN_TILE, TOKEN_TILE)), full((LANES, LANES))],
        out_specs=[tok(d), per_tile(SORT_ROWS, HALF_D), tok(LANES), per_tile(1, LANES)],
        input_output_aliases={0: 0},
        compiler_params=_cparams(("parallel", "arbitrary")),
        name="out_route",
    )(x, *ys, mod, mod, g.reshape(1, d), w_out, w_route, b_route, tri, upper)


def _split3_bf16(a):
    p1 = a.astype(BF16)
    r1 = a - p1.astype(F32)
    p2 = r1.astype(BF16)
    return p1, p2, (r1 - p2.astype(F32)).astype(BF16)


def _moe_meta_kernel(cnt_ref, tril_ref, upper_ref, src_ref, dst_ref, blk_ref, used_ref, *, n_blocks):
    n_tiles = cnt_ref.shape[0]
    cnt = cnt_ref[...].astype(F32)
    upper = upper_ref[...]
    tiles_before = _dot2_rhs(tril_ref[...], cnt)
    total = jnp.sum(cnt, 0, keepdims=True)
    padded = jnp.floor((total + (EXPERT_TILE - 1)) * (1.0 / EXPERT_TILE)) * EXPERT_TILE
    parts = _split3_bf16(jnp.broadcast_to(padded, (8, LANES)))
    start = (_dot(parts[0], upper) + _dot(parts[1], upper) + _dot(parts[2], upper))[0:1]
    tile_row = lax.broadcasted_iota(jnp.int32, (n_tiles, 1), 0).astype(F32)
    src_ref[...] = (tile_row * SORT_ROWS + _dot2_lhs(cnt, upper)).astype(jnp.int32)
    dst_ref[...] = (start + tiles_before).astype(jnp.int32)
    lane = lax.broadcasted_iota(jnp.int32, (1, LANES), 1)
    is_expert = (lane >= N_GROUPS) & (lane < N_GROUPS + N_EXPERTS)
    end = start + padded
    blk0 = lax.broadcasted_iota(jnp.int32, (n_blocks, 1), 0).astype(F32) * EXPERT_TILE
    done = jnp.sum(jnp.where(is_expert & (end <= blk0), 1.0, 0.0), -1, keepdims=True)
    blk_ref[...] = jnp.minimum(done, N_EXPERTS - 1.0).astype(jnp.int32)
    used_ref[...] = (jnp.sum(padded, -1, keepdims=True) * (1.0 / EXPERT_TILE)).astype(jnp.int32)


def _moe_meta(counts, tril_tiles, upper, n_blocks):
    n_tiles = counts.shape[0]
    sds = jax.ShapeDtypeStruct
    src, dst, blk, used = pl.pallas_call(
        functools.partial(_moe_meta_kernel, n_blocks=n_blocks),
        out_shape=[sds((n_tiles, LANES), jnp.int32), sds((n_tiles, LANES), jnp.int32),
                   sds((n_blocks, 1), jnp.int32), sds((1, 1), jnp.int32)],
        name="moe_meta",
    )(counts.reshape(n_tiles, LANES), tril_tiles, upper)
    flat = lambda tab: tab[:, N_GROUPS:N_GROUPS + N_EXPERTS].reshape(n_tiles * N_EXPERTS)
    return (flat(counts.reshape(n_tiles, LANES)), flat(src), flat(dst)), blk.reshape(n_blocks), used.reshape(1)


RUN_BITS = SORT_ROWS.bit_length()


def _run_copies(n, src, src_row, dst, dst_row, sem, fn):
    for b in range(RUN_BITS - 1, RUN_ALIGN.bit_length() - 2, -1):
        size = 1 << b
        done = lax.shift_left(lax.shift_right_logical(n, b + 1), b + 1)

        @pl.when((n & size) != 0)
        def _():
            fn(pltpu.make_async_copy(src.at[pl.ds(pl.multiple_of(src_row + done, RUN_ALIGN), size), :],
                                     dst.at[pl.ds(pl.multiple_of(dst_row + done, RUN_ALIGN), size), :], sem))


def _tile_runs(cnt_ref, src_ref, dst_ref, tile, sorted_ref, slots_ref, sem, fn, to_slots, sorted_base=0):
    for e in range(N_EXPERTS):
        k = tile * N_EXPERTS + e
        in_sorted = src_ref[k] - sorted_base
        if to_slots:
            _run_copies(cnt_ref[k], sorted_ref, in_sorted, slots_ref, dst_ref[k], sem, fn)
        else:
            _run_copies(cnt_ref[k], slots_ref, dst_ref[k], sorted_ref, in_sorted, sem, fn)


def _to_slots_kernel(cnt_ref, src_ref, dst_ref, hs_ref, zero_ref, xs_ref, sem):
    del zero_ref
    t = pl.program_id(0)
    runs = functools.partial(_tile_runs, cnt_ref, src_ref, dst_ref, tile=t, sorted_ref=hs_ref.at[0],
                             slots_ref=xs_ref, sem=sem, to_slots=True, sorted_base=t * SORT_ROWS)
    runs(fn=lambda cp: cp.start())
    runs(fn=lambda cp: cp.wait())


def _to_slots(tables, hs_sorted, n_slots):
    n_tiles = hs_sorted.shape[0]
    return pl.pallas_call(
        _to_slots_kernel,
        out_shape=jax.ShapeDtypeStruct((n_slots, HALF_D), jnp.int32),
        grid_spec=pltpu.PrefetchScalarGridSpec(
            num_scalar_prefetch=3,
            grid=(n_tiles,),
            in_specs=[pl.BlockSpec((1, SORT_ROWS, HALF_D), lambda t, *_: (t, 0, 0)),
                      pl.BlockSpec(memory_space=pl.ANY)],
            out_specs=pl.BlockSpec(memory_space=pl.ANY),
            scratch_shapes=[pltpu.SemaphoreType.DMA(())]),
        input_output_aliases={4: 0},
        compiler_params=_cparams(("arbitrary",)),
        name="moe_to_slots",
    )(*tables, hs_sorted, jnp.zeros((n_slots, HALF_D), jnp.int32))


def _expert_kernel(be_ref, nb_ref, x_ref, wg_ref, wu_ref, wd_ref, y_ref, wgu_s, wd_s):
    i = pl.program_id(0)

    @pl.when((i == 0) | (be_ref[i] != be_ref[jnp.maximum(i - 1, 0)]))
    def _():
        wgu_s[:, 0:D_EXPERT] = wg_ref[0, 0].astype(BF16)
        wgu_s[:, D_EXPERT:] = wu_ref[0, 0].astype(BF16)
        wd_s[...] = wd_ref[0, 0].astype(BF16)

    @pl.when(i < nb_ref[0])
    def _():
        lo, hi = _unpack_pairs(x_ref[...])
        gu = _dot(lo, wgu_s[0:HALF_D, :]) + _dot(hi, wgu_s[HALF_D:, :])
        hid = _silu(gu[:, 0:D_EXPERT]) * gu[:, D_EXPERT:]
        y_ref[...] = _pack_pairs(_dot(hid.astype(BF16), wd_s[...]))

    @pl.when(i >= nb_ref[0])
    def _():
        y_ref[...] = jnp.zeros_like(y_ref)


def _experts(blk_expert, n_used, xs, w_gate, w_up, w_down, layer):
    n_slots = xs.shape[0]
    weight = lambda rows, cols: pl.BlockSpec((1, 1, rows, cols), lambda i, be, nb: (layer, be[i], 0, 0))
    return pl.pallas_call(
        _expert_kernel,
        out_shape=jax.ShapeDtypeStruct((n_slots, HALF_D), jnp.int32),
        grid_spec=pltpu.PrefetchScalarGridSpec(
            num_scalar_prefetch=2,
            grid=(n_slots // EXPERT_TILE,),
            in_specs=[pl.BlockSpec((EXPERT_TILE, HALF_D), lambda i, be, nb: (i, 0)),
                      weight(D_MODEL, D_EXPERT), weight(D_MODEL, D_EXPERT), weight(D_EXPERT, D_MODEL)],
            out_specs=pl.BlockSpec((EXPERT_TILE, HALF_D), lambda i, be, nb: (i, 0)),
            scratch_shapes=[pltpu.VMEM((D_MODEL, 2 * D_EXPERT), BF16), pltpu.VMEM((D_EXPERT, D_MODEL), BF16)]),
        compiler_params=_cparams(("arbitrary",)),
        name="moe_experts",
    )(blk_expert, n_used, xs, w_gate, w_up, w_down)


def _combine_kernel(cnt_ref, src_ref, dst_ref, x_ref, route_ref, mb_ref, mc_ref, ys_ref, xo_ref, buf, sem,
                    *, n_lat):
    tiles = pl.num_programs(1)
    t = pl.program_id(0) * tiles + pl.program_id(1)
    buf[...] = jnp.zeros_like(buf)
    runs = functools.partial(_tile_runs, cnt_ref, src_ref, dst_ref, tile=t, sorted_ref=buf, slots_ref=ys_ref,
                             sem=sem, to_slots=False, sorted_base=t * SORT_ROWS)
    runs(fn=lambda cp: cp.start())
    route = route_ref[0]
    col = lax.broadcasted_iota(jnp.int32, (1, SORT_ROWS), 1).astype(F32)
    weights = (jnp.where(col == route[:, ROUTE_P0:ROUTE_P0 + 1], route[:, ROUTE_G0:ROUTE_G0 + 1], 0.0)
               + jnp.where(col == route[:, ROUTE_P1:ROUTE_P1 + 1], route[:, ROUTE_G1:ROUTE_G1 + 1], 0.0)
               ).astype(BF16)
    runs(fn=lambda cp: cp.wait())
    lo, hi = _unpack_pairs(buf[...])
    moe = jnp.concatenate([_dot(weights, lo), _dot(weights, hi)], 1)
    is_ctx = _is_ctx_rows(pl.program_id(1), TOKEN_TILE, n_lat)
    xo_ref[0] = x_ref[0] + _mod_vec(mb_ref, mc_ref, 5, is_ctx) * moe


def _combine(x, route, mod, tables, ys, n_lat):
    n_batch, s, d = x.shape
    tiles = s // TOKEN_TILE
    tok = lambda width: pl.BlockSpec((1, TOKEN_TILE, width), lambda b, j, *_: (b, j, 0))
    mod_specs = [pl.BlockSpec((1, 1, N_MOD * d), lambda b, j, *_: (b, 0, 0)),
                 pl.BlockSpec((1, 1, N_MOD * d), lambda b, j, *_: (n_batch, 0, 0))]
    return pl.pallas_call(
        functools.partial(_combine_kernel, n_lat=n_lat),
        out_shape=jax.ShapeDtypeStruct((n_batch, s, d), F32),
        grid_spec=pltpu.PrefetchScalarGridSpec(
            num_scalar_prefetch=3,
            grid=(n_batch, tiles),
            in_specs=[tok(d), tok(LANES)] + mod_specs + [pl.BlockSpec(memory_space=pl.ANY)],
            out_specs=tok(d),
            scratch_shapes=[pltpu.VMEM((SORT_ROWS, HALF_D), jnp.int32), pltpu.SemaphoreType.DMA(())]),
        input_output_aliases={3: 0},
        compiler_params=_cparams(("arbitrary", "arbitrary")),
        name="moe_combine",
    )(*tables, x, route, mod, mod, ys)


def _final_kernel(x_ref, g_ref, o_ref):
    x = x_ref[0]
    o_ref[0] = x * lax.rsqrt(jnp.mean(x * x, -1, keepdims=True) + EPS) * g_ref[...]


def _final_norm(x, g, n_lat):
    n_batch, _, d = x.shape
    tm = 512
    return pl.pallas_call(
        _final_kernel,
        out_shape=jax.ShapeDtypeStruct((n_batch, n_lat, d), F32),
        grid=(n_batch, n_lat // tm),
        in_specs=[pl.BlockSpec((1, tm, d), lambda b, j: (b, j, 0)), pl.BlockSpec((1, d), lambda b, j: (0, 0))],
        out_specs=pl.BlockSpec((1, tm, d), lambda b, j: (b, j, 0)),
        compiler_params=_cparams(("parallel", "arbitrary")),
        name="final_norm",
    )(x, g.reshape(1, d))


def _block_diag(w):
    heads, di, dj = w.shape
    eye = jnp.eye(heads, dtype=w.dtype)
    return (eye[:, None, :, None] * w[:, :, None, :]).reshape(heads * di, heads * dj)


def _rope_tables(n_lat):
    rows = n_lat // GRID_W
    row = np.repeat(np.arange(rows, dtype=np.float64), GRID_W)
    col = np.tile(np.arange(GRID_W, dtype=np.float64), rows)
    inv = ROPE_BASE ** (-np.arange(ROPE_FREQS, dtype=np.float64) / ROPE_FREQS)
    ang = np.concatenate([row[:, None] * inv, col[:, None] * inv], -1)
    cos = np.tile(np.concatenate([np.cos(ang), np.cos(ang)], -1), (1, W_MIX // DIFF_DK))
    sin = np.tile(np.concatenate([-np.sin(ang), np.sin(ang)], -1), (1, W_MIX // DIFF_DK))
    return jnp.asarray(cos, F32), jnp.asarray(sin, F32)


def kernel(x, c, ctx, c_ctx, w_ada, b_ada, g_mix, g_ffn, w_in, w_out, a_conv_w, a_conv_b, a_w_r, a_b_r,
           a_w_i, a_b_i, a_lam, b_lq1, b_lk1, b_lq2, b_lk2, b_sub_g, c_conv_w, c_conv_b, c_ln_g, c_ln_b,
           c_w_pw, c_b_pw, d_conv_w, d_conv_b, d_w_f1, d_b_f1, d_freq, d_w_f2, d_b_f2, d_w_f3, d_decay,
           d_bias, moe_w_rg, moe_b_rg, moe_w_re, moe_b_re, moe_w_gate, moe_w_up, moe_w_down, g_final):
    n_batch, n_lat, d = x.shape
    n_ctx = ctx.shape[1]
    depth = w_ada.shape[0]
    s = n_lat + n_ctx
    assert d == D_MODEL and s % TOKEN_TILE == 0 and n_lat % CONV_BLOCK == 0 and n_ctx <= CONV_BLOCK
    assert n_lat % Q_TILE == 0 and n_ctx % LANES == 0 and n_lat % GRID_W == 0

    xs = jnp.concatenate([x, ctx], 1)
    mod_rows = -(-(n_batch + 1) // MOD_ROWS_PAD) * MOD_ROWS_PAD
    c_all = jnp.concatenate([c, c_ctx[None], jnp.zeros((mod_rows - n_batch - 1, d), F32)], 0)
    mod_all = _ada_table(c_all, w_ada, b_ada)

    rope_cos, rope_sin = _rope_tables(n_lat)
    fwd_np, inv_np, filt_np = _dft_matrices()
    fwd_m, inv_m, filt_m = (jnp.asarray(m, F32).astype(BF16) for m in (fwd_np, inv_np, filt_np))
    feats_l, aux_l = (jnp.asarray(m) for m in _hyena_positions(2 * n_lat, n_lat, n_lat))
    feats_c, aux_c = (jnp.asarray(m) for m in _hyena_positions(CONV_FFT, CONV_BLOCK, n_ctx))
    gmat = jnp.asarray(np.kron(np.eye(DIFF_HEADS), np.full((DIFF_DV, DIFF_DV), 1.0 / DIFF_DV)), F32).astype(BF16)
    tri = jnp.asarray(np.tril(np.ones((TOKEN_TILE, TOKEN_TILE)), -1), F32).astype(BF16)
    n_tiles = n_batch * (s // TOKEN_TILE)
    tril_tiles = jnp.asarray(np.tril(np.ones((n_tiles, n_tiles)), -1), F32).astype(BF16)
    upper = jnp.asarray(np.triu(np.ones((LANES, LANES)), 1), F32).astype(BF16)

    w1 = jnp.pad(d_w_f1, ((0, 0), (0, LANES - HYENA_EMB), (0, 0)))
    w3 = d_w_f3.reshape(depth, HYENA_FFN, HYENA_ORDER, 2 * W_MIX).transpose(0, 2, 1, 3)
    dec = d_decay.reshape(depth, HYENA_ORDER, 1, 2 * W_MIX)
    row = lambda v: v.reshape(depth, 1, -1)
    hy = _hyena_filters(feats_l, aux_l, feats_c, aux_c, w1, row(d_b_f1), row(d_freq), d_w_f2, row(d_b_f2),
                        w3, dec, filt_m)

    n_tok = n_batch * s
    for l in range(depth):
        last = l == depth - 1
        lam_init = 0.8 - 0.6 * math.exp(-0.3 * l)
        mod = mod_all[l].reshape(mod_rows, 1, N_MOD * d)
        p_a, p_b, p_c, p_d = _in_proj(xs, mod, g_mix[l], w_in[l].astype(BF16), n_lat)

        w_gates = jnp.stack([jnp.concatenate([_block_diag(a_w_r[l, dr]), _block_diag(a_w_i[l, dr])], 1)
                             for dr in range(2)]).astype(BF16)
        b_gates = jnp.concatenate([a_b_r[l], a_b_i[l]], -1)[:, None, :]
        y_a = _rglru(p_a, a_conv_w[l], a_conv_b[l][:, None, :], w_gates, b_gates, a_lam[l][:, None, :], n_lat)
        lqk = jnp.stack([b_lq1[l], b_lk1[l], b_lq2[l], b_lk2[l]])
        y_b = _diff_attn(p_b, rope_cos, rope_sin, lqk, jnp.tile(b_sub_g[l], DIFF_HEADS)[None], gmat,
                         n_lat, lam_init)
        y_c = _conformer(p_c, c_conv_w[l], c_conv_b[l][None], c_ln_g[l][None], c_ln_b[l][None],
                         c_w_pw[l].astype(BF16), c_b_pw[l][None], n_lat)
        y_d = _hyena(p_d, d_conv_w[l], d_conv_b[l][None], d_bias[l][:, None, :], [f[l] for f in hy],
                     fwd_m, inv_m, n_lat)

        w_route = jnp.pad(jnp.concatenate([moe_w_rg[l], moe_w_re[l]], 1),
                          ((0, 0), (0, LANES - N_GROUPS - N_EXPERTS)))
        b_route = jnp.pad(jnp.concatenate([moe_b_rg[l], moe_b_re[l]]), (0, LANES - N_GROUPS - N_EXPERTS))[None]
        xs, hs_sorted, route, counts = _out_route(xs, (y_a, y_b, y_c, y_d), mod, g_ffn[l], w_out[l].astype(BF16),
                                                  w_route, b_route, tri, upper, n_lat, not last)

        n_assign = 2 * (n_tok if not last else n_batch * n_lat)
        n_rows_max = n_assign + n_tiles * N_EXPERTS * (RUN_ALIGN - 1)
        n_slots = -(-n_rows_max // EXPERT_TILE) * EXPERT_TILE + N_EXPERTS * EXPERT_TILE
        tables, blk_expert, n_used = _moe_meta(counts, tril_tiles, upper, n_slots // EXPERT_TILE)
        slots = _to_slots(tables, hs_sorted, n_slots)
        ys = _experts(blk_expert, n_used, slots, moe_w_gate, moe_w_up, moe_w_down, l)
        xs = _combine(xs, route, mod, tables, ys, n_lat)

    return _final_norm(xs, g_final, n_lat)
```
